```python
import jax, jax.numpy as jnp
from jax import lax
import numpy as np

D_MODEL = 2048
BATCH = 4
SEQ = 2048
DEPTH = 1

SSM_EXPAND = 2
SSM_D_INNER = SSM_EXPAND * D_MODEL
SSM_HEAD_DIM = 64
SSM_N_HEADS = SSM_D_INNER // SSM_HEAD_DIM
SSM_N_GROUPS = 8
SSM_D_STATE = 128
SSM_CONV = 4
SSM_CHUNK = 256
SSM_CONV_DIM = SSM_D_INNER + 2 * SSM_N_GROUPS * SSM_D_STATE
ATTN_HEAD_DIM = 64
ATTN_N_HEADS = D_MODEL // ATTN_HEAD_DIM
ATTN_N_KV = ATTN_N_HEADS // 8
ATTN_Q_PER_KV = ATTN_N_HEADS // ATTN_N_KV
WINDOW = 128
ATTN_BLOCK = 128
D_FF = 5632
EPS = 1e-6

IN_SIZES = (
    SSM_D_INNER,
    SSM_CONV_DIM,
    SSM_N_HEADS,
    ATTN_N_HEADS * ATTN_HEAD_DIM,
    ATTN_N_KV * ATTN_HEAD_DIM,
    ATTN_N_KV * ATTN_HEAD_DIM,
    D_MODEL,
    D_MODEL,
)
IN_COLS = int(sum(IN_SIZES))
IN_SPLITS = tuple(int(s) for s in np.cumsum(IN_SIZES)[:-1])

kernel_name = "hybrid_ssd_swa_macaron_block"


def rms_norm(x, w):
    xf = x.astype(jnp.float32)
    y = xf * lax.rsqrt(jnp.mean(xf * xf, axis=-1, keepdims=True) + EPS)
    return (y * w.astype(jnp.float32)).astype(x.dtype)


def swiglu(h, w_gate, w_up, w_down):
    return (jax.nn.silu(h @ w_gate) * (h @ w_up)) @ w_down


def causal_depthwise_conv(u, w, b):
    k = w.shape[0]
    out = lax.conv_general_dilated(
        u, w[:, None, :].astype(u.dtype), window_strides=(1,), padding=[(k - 1, 0)],
        dimension_numbers=("NWC", "WIO", "NWC"), feature_group_count=u.shape[-1])
    return out + b


def ssd_chunked(xs, dt, a, bmat, cmat):
    bsz, t_len, n_h, p_dim = xs.shape
    g, n = bmat.shape[2], bmat.shape[3]
    r = n_h // g
    L = SSM_CHUNK
    nc = -(-t_len // L)
    pad = nc * L - t_len
    xs, dt, bmat, cmat = (v.astype(jnp.float32) for v in (xs, dt, bmat, cmat))
    if pad:
        xs = jnp.pad(xs, ((0, 0), (0, pad), (0, 0), (0, 0)))
        dt = jnp.pad(dt, ((0, 0), (0, pad), (0, 0)))
        bmat = jnp.pad(bmat, ((0, 0), (0, pad), (0, 0), (0, 0)))
        cmat = jnp.pad(cmat, ((0, 0), (0, pad), (0, 0), (0, 0)))
    xdt = (xs * dt[..., None]).reshape(bsz, nc, L, g, r, p_dim)
    adt = (dt * a.astype(jnp.float32)).reshape(bsz, nc, L, g, r)
    a_cum = jnp.cumsum(jnp.transpose(adt, (0, 1, 3, 4, 2)), axis=-1)
    bc = bmat.reshape(bsz, nc, L, g, n)
    cc = cmat.reshape(bsz, nc, L, g, n)
    causal = jnp.asarray(np.tril(np.ones((L, L), dtype=bool)))
    seg = a_cum[..., :, None] - a_cum[..., None, :]
    cb = jnp.einsum("bclgn,bcsgn->bcgls", cc, bc)
    w_ls = cb[:, :, :, None] * jnp.exp(jnp.where(causal, seg, -jnp.inf))
    y_diag = jnp.einsum("bcgrls,bcsgrp->bclgrp", w_ls, xdt)
    decay_to_end = jnp.exp(a_cum[..., -1:] - a_cum)
    chunk_states = jnp.einsum("bclgn,bcgrl,bclgrp->bcgrpn", bc, decay_to_end, xdt)
    chunk_decay = jnp.exp(a_cum[..., -1])

    def step(state, inp):
        s_c, d_c = inp
        return state * d_c[..., None, None] + s_c, state

    init = jnp.zeros((bsz, g, r, p_dim, n), jnp.float32)
    _, prev = lax.scan(step, init, (jnp.moveaxis(chunk_states, 1, 0), jnp.moveaxis(chunk_decay, 1, 0)))
    prev = jnp.moveaxis(prev, 0, 1)
    y_off = jnp.einsum("bclgn,bcgrpn,bcgrl->bclgrp", cc, prev, jnp.exp(a_cum))
    y = (y_diag + y_off).reshape(bsz, nc * L, n_h, p_dim)
    return y[:, :t_len]


def ssd_branch(z, xbc, dt_raw, conv_w, conv_b, dt_bias, a_log, d_skip, ssm_norm):
    bsz, t_len, _ = z.shape
    xbc = jax.nn.silu(causal_depthwise_conv(xbc, conv_w, conv_b))
    gn = SSM_N_GROUPS * SSM_D_STATE
    xs = xbc[..., :SSM_D_INNER].reshape(bsz, t_len, SSM_N_HEADS, SSM_HEAD_DIM)
    bm = xbc[..., SSM_D_INNER:SSM_D_INNER + gn].reshape(bsz, t_len, SSM_N_GROUPS, SSM_D_STATE)
    cm = xbc[..., SSM_D_INNER + gn:].reshape(bsz, t_len, SSM_N_GROUPS, SSM_D_STATE)
    dt = jax.nn.softplus(dt_raw.astype(jnp.float32) + dt_bias.astype(jnp.float32))
    a = -jnp.exp(a_log.astype(jnp.float32))
    y = ssd_chunked(xs, dt, a, bm, cm) + d_skip.astype(jnp.float32)[:, None] * xs.astype(jnp.float32)
    y = y.reshape(bsz, t_len, SSM_D_INNER)
    yg = (y * jax.nn.silu(z.astype(jnp.float32))).reshape(bsz, t_len, SSM_N_GROUPS, SSM_D_INNER // SSM_N_GROUPS)
    yg = yg * lax.rsqrt(jnp.mean(yg * yg, axis=-1, keepdims=True) + EPS)
    return (yg.reshape(bsz, t_len, SSM_D_INNER) * ssm_norm.astype(jnp.float32)).astype(z.dtype)


def alibi_slopes(n_heads):
    return np.array([2.0 ** (-8.0 * (h + 1) / n_heads) for h in range(n_heads)], dtype=np.float32)


def swa_branch(q, k, v, q_norm, k_norm, sinks):
    bsz, t_len, _ = q.shape
    blk = ATTN_BLOCK
    nb = t_len // blk
    q = rms_norm(q.reshape(bsz, t_len, ATTN_N_KV, ATTN_Q_PER_KV, ATTN_HEAD_DIM), q_norm)
    k = rms_norm(k.reshape(bsz, t_len, ATTN_N_KV, ATTN_HEAD_DIM), k_norm)
    v = v.reshape(bsz, t_len, ATTN_N_KV, ATTN_HEAD_DIM)
    qb = q.reshape(bsz, nb, blk, ATTN_N_KV, ATTN_Q_PER_KV, ATTN_HEAD_DIM)
    zpad = ((0, 0), (blk, 0), (0, 0), (0, 0))
    kp = jnp.pad(k, zpad).reshape(bsz, nb + 1, blk, ATTN_N_KV, ATTN_HEAD_DIM)
    vp = jnp.pad(v, zpad).reshape(bsz, nb + 1, blk, ATTN_N_KV, ATTN_HEAD_DIM)
    kband = jnp.concatenate([kp[:, :-1], kp[:, 1:]], axis=2)
    vband = jnp.concatenate([vp[:, :-1], vp[:, 1:]], axis=2)
    scale = ATTN_HEAD_DIM ** -0.5
    s = jnp.einsum("bnqkgd,bnskd->bnkgqs", qb, kband).astype(jnp.float32) * scale
    qi = np.arange(blk)[:, None]
    sj = np.arange(2 * blk)[None, :]
    dist = (qi + blk - sj).astype(np.float32)
    key_pos = np.arange(nb)[:, None, None] * blk - blk + sj[None]
    valid = (dist >= 0) & (dist < WINDOW) & (key_pos >= 0)
    slopes = jnp.asarray(alibi_slopes(ATTN_N_HEADS)).reshape(ATTN_N_KV, ATTN_Q_PER_KV)
    s = s - slopes[:, :, None, None] * jnp.asarray(dist)
    s = jnp.where(jnp.asarray(valid)[None, :, None, None], s, -jnp.inf)
    sink = sinks.astype(jnp.float32).reshape(ATTN_N_KV, ATTN_Q_PER_KV)[None, None, :, :, None]
    m = jnp.maximum(jnp.max(s, axis=-1), sink)
    p = jnp.exp(s - m[..., None])
    denom = jnp.sum(p, axis=-1) + jnp.exp(sink - m)
    p = (p / denom[..., None]).astype(v.dtype)
    o = jnp.einsum("bnkgqs,bnskd->bnqkgd", p, vband)
    return o.reshape(bsz, t_len, ATTN_N_HEADS * ATTN_HEAD_DIM)


def hybrid_mixer(h, w_in, conv_w, conv_b, dt_bias, a_log, d_skip, ssm_norm,
                 q_norm, k_norm, sinks, w_o_ssm, w_o_attn, w_out):
    proj = h @ w_in
    z, xbc, dt_raw, q, k, v, g_ssm, g_attn = jnp.split(proj, IN_SPLITS, axis=-1)
    y_ssm = ssd_branch(z, xbc, dt_raw, conv_w, conv_b, dt_bias, a_log, d_skip, ssm_norm)
    y_attn = swa_branch(q, k, v, q_norm, k_norm, sinks)
    merged = jax.nn.sigmoid(g_ssm) * (y_ssm @ w_o_ssm) + jax.nn.sigmoid(g_attn) * (y_attn @ w_o_attn)
    return merged @ w_out


def setup_inputs(seed: int = 0) -> dict:
    key = jax.random.key(seed)
    ks = jax.random.split(key, 24)
    f32 = jnp.float32

    def normal(k, shape, fan_in):
        return jax.random.normal(k, shape, f32) * fan_in ** -0.5

    def gain(k, shape):
        return 1.0 + 0.02 * jax.random.normal(k, shape, f32)

    dL = DEPTH
    dt_init = jnp.exp(jax.random.uniform(ks[9], (dL, SSM_N_HEADS), f32, np.log(1e-3), np.log(1e-1)))
    return {
        "x": jax.random.normal(ks[0], (BATCH, SEQ, D_MODEL), f32),
        "ffn1_norm": gain(ks[1], (dL, D_MODEL)),
        "ffn1_w_gate": normal(ks[2], (dL, D_MODEL, D_FF), D_MODEL),
        "ffn1_w_up": normal(ks[3], (dL, D_MODEL, D_FF), D_MODEL),
        "ffn1_w_down": normal(ks[4], (dL, D_FF, D_MODEL), D_FF),
        "mix_norm": gain(ks[5], (dL, D_MODEL)),
        "w_in": normal(ks[6], (dL, D_MODEL, IN_COLS), D_MODEL),
        "conv_w": normal(ks[7], (dL, SSM_CONV, SSM_CONV_DIM), SSM_CONV),
        "conv_b": 0.02 * jax.random.normal(ks[8], (dL, SSM_CONV_DIM), f32),
        "dt_bias": dt_init + jnp.log(-jnp.expm1(-dt_init)),
        "a_log": jnp.log(jax.random.uniform(ks[10], (dL, SSM_N_HEADS), f32, 1.0, 16.0)),
        "d_skip": gain(ks[11], (dL, SSM_N_HEADS)),
        "ssm_norm": gain(ks[12], (dL, SSM_D_INNER)),
        "q_norm": gain(ks[13], (dL, ATTN_HEAD_DIM)),
        "k_norm": gain(ks[14], (dL, ATTN_HEAD_DIM)),
        "sinks": 0.5 * jax.random.normal(ks[15], (dL, ATTN_N_HEADS), f32),
        "w_o_ssm": normal(ks[16], (dL, SSM_D_INNER, D_MODEL), SSM_D_INNER),
        "w_o_attn": normal(ks[17], (dL, ATTN_N_HEADS * ATTN_HEAD_DIM, D_MODEL), ATTN_N_HEADS * ATTN_HEAD_DIM),
        "w_out": normal(ks[18], (dL, D_MODEL, D_MODEL), D_MODEL),
        "ffn2_norm": gain(ks[19], (dL, D_MODEL)),
        "ffn2_w_gate": normal(ks[20], (dL, D_MODEL, D_FF), D_MODEL),
        "ffn2_w_up": normal(ks[21], (dL, D_MODEL, D_FF), D_MODEL),
        "ffn2_w_down": normal(ks[22], (dL, D_FF, D_MODEL), D_FF),
    }


def reference(x, ffn1_norm, ffn1_w_gate, ffn1_w_up, ffn1_w_down, mix_norm, w_in, conv_w, conv_b,
              dt_bias, a_log, d_skip, ssm_norm, q_norm, k_norm, sinks, w_o_ssm, w_o_attn, w_out,
              ffn2_norm, ffn2_w_gate, ffn2_w_up, ffn2_w_down):
    for l in range(DEPTH):
        x = x + 0.5 * swiglu(rms_norm(x, ffn1_norm[l]), ffn1_w_gate[l], ffn1_w_up[l], ffn1_w_down[l])
        x = x + hybrid_mixer(rms_norm(x, mix_norm[l]), w_in[l], conv_w[l], conv_b[l], dt_bias[l],
                             a_log[l], d_skip[l], ssm_norm[l], q_norm[l], k_norm[l], sinks[l],
                             w_o_ssm[l], w_o_attn[l], w_out[l])
        x = x + 0.5 * swiglu(rms_norm(x, ffn2_norm[l]), ffn2_w_gate[l], ffn2_w_up[l], ffn2_w_down[l])
    return x
```

```python
import functools

import jax
import jax.numpy as jnp
import numpy as np
from jax import lax
from jax.experimental import pallas as pl
from jax.experimental.pallas import tpu as pltpu

F32 = jnp.float32
BF16 = jnp.bfloat16

D_MODEL = 2048
SSM_D_INNER = 4096
SSM_HEAD_DIM = 64
SSM_N_HEADS = 64
SSM_N_GROUPS = 8
SSM_HEADS_PER_GROUP = SSM_N_HEADS // SSM_N_GROUPS
SSM_GROUP_WIDTH = SSM_D_INNER // SSM_N_GROUPS
SSM_D_STATE = 128
SSM_CONV = 4
SSM_CHUNK = 256
SSM_GN = SSM_N_GROUPS * SSM_D_STATE
ATTN_HEAD_DIM = 64
ATTN_N_HEADS = 32
ATTN_N_KV = 4
ATTN_Q_PER_KV = 8
ATTN_D = ATTN_N_HEADS * ATTN_HEAD_DIM
ATTN_KV_D = ATTN_N_KV * ATTN_HEAD_DIM
WINDOW = 128
ATTN_BLOCK = 128
D_FF = 5632
EPS = 1e-6
NEG = -1e30

LANES = 128
CONV_HALO = 8

P_Z = 0
P_X = P_Z + SSM_D_INNER
P_B = P_X + SSM_D_INNER
P_C = P_B + SSM_GN
P_Q = P_C + SSM_GN
P_K = P_Q + ATTN_D
P_V = P_K + ATTN_KV_D
P_GS = P_V + ATTN_KV_D
P_GA = P_GS + D_MODEL
P_COLS = P_GA + D_MODEL

W_Z = 0
W_XBC = W_Z + SSM_D_INNER
W_DT = W_XBC + SSM_D_INNER + 2 * SSM_GN
W_Q = W_DT + SSM_N_HEADS
W_K = W_Q + ATTN_D
W_V = W_K + ATTN_KV_D
W_GS = W_V + ATTN_KV_D
W_GA = W_GS + D_MODEL

VMEM_LIMIT = 56 * 1024 * 1024


def _params(sem):
    return pltpu.CompilerParams(dimension_semantics=sem, vmem_limit_bytes=VMEM_LIMIT)


def _rms(x, gain):
    return x * lax.rsqrt(jnp.mean(x * x, axis=-1, keepdims=True) + EPS) * gain


def _silu(x):
    return x * jax.nn.sigmoid(x)


def _dot(a, b):
    return jnp.dot(a, b, preferred_element_type=F32)


def _dot_nt(a, b):
    return lax.dot_general(a, b, (((1,), (1,)), ((), ())), preferred_element_type=F32)


def _split3(x):
    hi = x.astype(BF16)
    r1 = x - hi.astype(F32)
    mid = r1.astype(BF16)
    lo = (r1 - mid.astype(F32)).astype(BF16)
    return hi, mid, lo


def _sel_left(m01, x):
    hi, mid, lo = _split3(x)
    return (_dot(m01, lo) + _dot(m01, mid)) + _dot(m01, hi)


def _sel_right(x, m01):
    hi, mid, lo = _split3(x)
    return (_dot(lo, m01) + _dot(mid, m01)) + _dot(hi, m01)


def _ffn_kernel(x_ref, gain_ref, wg_ref, wu_ref, wd_ref, o_ref, h_ref):
    j = pl.program_id(1)

    @pl.when(j == 0)
    def _():
        x = x_ref[...]
        h_ref[...] = _rms(x, gain_ref[...]).astype(BF16)
        o_ref[...] = x

    h = h_ref[...]
    g = _dot(h, wg_ref[...])
    u = _dot(h, wu_ref[...])
    a = (0.5 * _silu(g) * u).astype(BF16)
    o_ref[...] += _dot(a, wd_ref[...])


def _ffn(x, gain, wg, wu, wd, tm, tf):
    m, d = x.shape
    dff = wg.shape[1]
    return pl.pallas_call(
        _ffn_kernel,
        grid=(m // tm, dff // tf),
        in_specs=[
            pl.BlockSpec((tm, d), lambda i, j: (i, 0)),
            pl.BlockSpec((1, d), lambda i, j: (0, 0)),
            pl.BlockSpec((d, tf), lambda i, j: (0, j)),
            pl.BlockSpec((d, tf), lambda i, j: (0, j)),
            pl.BlockSpec((tf, d), lambda i, j: (j, 0)),
        ],
        out_specs=pl.BlockSpec((tm, d), lambda i, j: (i, 0)),
        out_shape=jax.ShapeDtypeStruct((m, d), F32),
        scratch_shapes=[pltpu.VMEM((tm, d), BF16)],
        compiler_params=_params(("parallel", "arbitrary")),
        name="ffn",
    )(x, gain, wg, wu, wd)


def _inproj_kernel(x_ref, gain_ref, w_ref, wdt_ref, p_ref, dt_ref, h_ref):
    j = pl.program_id(1)

    @pl.when(j == 0)
    def _():
        h = _rms(x_ref[...], gain_ref[...]).astype(BF16)
        h_ref[...] = h
        dt_ref[...] = _dot(h, wdt_ref[...])

    p_ref[...] = _dot(h_ref[...], w_ref[...]).astype(BF16)


def _inproj(x, gain, w, wdt, tm, tn):
    m, d = x.shape
    n = w.shape[1]
    return pl.pallas_call(
        _inproj_kernel,
        grid=(m // tm, n // tn),
        in_specs=[
            pl.BlockSpec((tm, d), lambda i, j: (i, 0)),
            pl.BlockSpec((1, d), lambda i, j: (0, 0)),
            pl.BlockSpec((d, tn), lambda i, j: (0, j)),
            pl.BlockSpec((d, LANES), lambda i, j: (0, 0)),
        ],
        out_specs=[
            pl.BlockSpec((tm, tn), lambda i, j: (i, j)),
            pl.BlockSpec((tm, LANES), lambda i, j: (i, 0)),
        ],
        out_shape=[
            jax.ShapeDtypeStruct((m, n), BF16),
            jax.ShapeDtypeStruct((m, LANES), F32),
        ],
        scratch_shapes=[pltpu.VMEM((tm, d), BF16)],
        compiler_params=_params(("parallel", "arbitrary")),
        name="in_proj",
    )(x, gain, w, wdt)


def _ssd_kernel(z_ref, x_ref, b_ref, c_ref, dt_ref,
                cwx_ref, cwb_ref, cwc_ref, cbx_ref, cbb_ref, cbc_ref,
                dtb_ref, alog_ref, alogx_ref, dsk_ref, nrm_ref, e_ref, selt_ref,
                o_ref, ext_ref, state_ref):
    c = pl.program_id(2)
    L = SSM_CHUNK
    GW = SSM_GROUP_WIDTH
    N = SSM_D_STATE
    XBC = GW + 2 * N

    @pl.when(c == 0)
    def _():
        ext_ref[0:CONV_HALO, :] = jnp.zeros((CONV_HALO, XBC), F32)
        state_ref[...] = jnp.zeros_like(state_ref)

    ext_ref[CONV_HALO:CONV_HALO + L, 0:GW] = x_ref[...].astype(F32)
    ext_ref[CONV_HALO:CONV_HALO + L, GW:GW + N] = b_ref[...].astype(F32)
    ext_ref[CONV_HALO:CONV_HALO + L, GW + N:XBC] = c_ref[...].astype(F32)

    cw = jnp.concatenate([cwx_ref[...], cwb_ref[...], cwc_ref[...]], axis=1)
    cbias = jnp.concatenate([cbx_ref[...], cbb_ref[...], cbc_ref[...]], axis=1)
    acc = cbias + cw[0:1, :] * ext_ref[CONV_HALO - 3:CONV_HALO - 3 + L, :]
    for k in range(1, SSM_CONV):
        off = CONV_HALO - (SSM_CONV - 1) + k
        acc = acc + cw[k:k + 1, :] * ext_ref[off:off + L, :]
    ext_ref[0:CONV_HALO, :] = ext_ref[L:L + CONV_HALO, :]
    xbc = _silu(acc)
    xs = xbc[:, 0:GW]
    bm = xbc[:, GW:GW + N]
    cm = xbc[:, GW + N:XBC]

    dtr = dt_ref[...] + dtb_ref[...]
    dt = jnp.maximum(dtr, 0.0) + jnp.log1p(jnp.exp(-jnp.abs(dtr)))
    e01 = e_ref[...]
    dt_x = _sel_right(dt, e01)
    adt_c = dt * (-jnp.exp(alog_ref[...]))
    adt_x = dt_x * (-jnp.exp(alogx_ref[...]))
    row = lax.broadcasted_iota(jnp.int32, (L, L), 0)
    col = lax.broadcasted_iota(jnp.int32, (L, L), 1)
    tril = row >= col
    tril01 = tril.astype(BF16)
    acum_x = _sel_left(tril01, adt_x)
    acum_c = _sel_left(tril01, adt_c)
    selt = selt_ref[...]
    hi, mid, lo = _split3(acum_c)
    acum_r = (_dot_nt(selt, lo) + _dot_nt(selt, mid)) + _dot_nt(selt, hi)

    bt = bm.T.astype(BF16)
    cmb = cm.astype(BF16)
    cbm = _dot(cmb, bt)
    xdt = xs * dt_x
    xdt_b = xdt.astype(BF16)

    lane = lax.broadcasted_iota(jnp.int32, (L, LANES), 1)
    low_half = lane < SSM_HEAD_DIM
    pairs = []
    for pair in range(SSM_HEADS_PER_GROUP // 2):
        xp = xdt_b[:, pair * LANES:(pair + 1) * LANES]
        y_pair = None
        for half in range(2):
            h = 2 * pair + half
            seg = acum_x[:, h * SSM_HEAD_DIM:h * SSM_HEAD_DIM + 1] - acum_r[h:h + 1, :]
            w = cbm * jnp.exp(jnp.where(tril, seg, NEG))
            keep = low_half if half == 0 else jnp.logical_not(low_half)
            xh = jnp.where(keep, xp, jnp.zeros_like(xp))
            yh = _dot(w.astype(BF16), xh)
            y_pair = yh if y_pair is None else y_pair + yh
        pairs.append(y_pair)
    y_diag = jnp.concatenate(pairs, axis=1)

    state = state_ref[...]
    y_off = jnp.exp(acum_x) * _dot(cmb, state.astype(BF16))
    y = y_diag + y_off + dsk_ref[...] * xs

    last = acum_x[L - 1:L, :]
    xdec = (xdt * jnp.exp(last - acum_x)).astype(BF16)
    state_ref[...] = state * jnp.exp(last) + _dot(bt, xdec)

    yg = y * _silu(z_ref[...].astype(F32))
    o_ref[...] = _rms(yg, nrm_ref[...]).astype(BF16)


def _ssd(p, dt_raw, conv_w, conv_b, dtb_row, alog_row, alog_x, dskip_x, nrm_row, e01, selt, bsz, t_len):
    L = SSM_CHUNK
    nc = t_len // L
    GW = SSM_GROUP_WIDTH
    N = SSM_D_STATE
    rowblk = lambda b, g, c: b * nc + c
    return pl.pallas_call(
        _ssd_kernel,
        grid=(bsz, SSM_N_GROUPS, nc),
        in_specs=[
            pl.BlockSpec((L, GW), lambda b, g, c: (rowblk(b, g, c), P_Z // GW + g)),
            pl.BlockSpec((L, GW), lambda b, g, c: (rowblk(b, g, c), P_X // GW + g)),
            pl.BlockSpec((L, N), lambda b, g, c: (rowblk(b, g, c), P_B // N + g)),
            pl.BlockSpec((L, N), lambda b, g, c: (rowblk(b, g, c), P_C // N + g)),
            pl.BlockSpec((L, LANES), lambda b, g, c: (rowblk(b, g, c), 0)),
            pl.BlockSpec((SSM_CONV, GW), lambda b, g, c: (0, g)),
            pl.BlockSpec((SSM_CONV, N), lambda b, g, c: (0, SSM_D_INNER // N + g)),
            pl.BlockSpec((SSM_CONV, N), lambda b, g, c: (0, (SSM_D_INNER + SSM_GN) // N + g)),
            pl.BlockSpec((1, GW), lambda b, g, c: (0, g)),
            pl.BlockSpec((1, N), lambda b, g, c: (0, SSM_D_INNER // N + g)),
            pl.BlockSpec((1, N), lambda b, g, c: (0, (SSM_D_INNER + SSM_GN) // N + g)),
            pl.BlockSpec((1, LANES), lambda b, g, c: (0, 0)),
            pl.BlockSpec((1, LANES), lambda b, g, c: (0, 0)),
            pl.BlockSpec((1, GW), lambda b, g, c: (0, g)),
            pl.BlockSpec((1, GW), lambda b, g, c: (0, g)),
            pl.BlockSpec((1, GW), lambda b, g, c: (0, g)),
            pl.BlockSpec((None, LANES, GW), lambda b, g, c: (g, 0, 0)),
            pl.BlockSpec((None, SSM_HEADS_PER_GROUP, LANES), lambda b, g, c: (g, 0, 0)),
        ],
        out_specs=pl.BlockSpec((L, GW), lambda b, g, c: (rowblk(b, g, c), g)),
        out_shape=jax.ShapeDtypeStruct((bsz * t_len, SSM_D_INNER), BF16),
        scratch_shapes=[
            pltpu.VMEM((CONV_HALO + L, GW + 2 * N), F32),
            pltpu.VMEM((N, GW), F32),
        ],
        compiler_params=_params(("parallel", "parallel", "arbitrary")),
        name="ssd",
    )(p, p, p, p, dt_raw, conv_w, conv_w, conv_w, conv_b, conv_b, conv_b,
      dtb_row, alog_row, alog_x, dskip_x, nrm_row, e01, selt)


def _attn_kernel(sink_ref, q_ref, kc_ref, kp_ref, vc_ref, vp_ref, qn_ref, kn_ref, o_ref):
    n = pl.program_id(1)
    blk = ATTN_BLOCK
    hd = ATTN_HEAD_DIM
    kf = jnp.concatenate([kp_ref[...], kc_ref[...]], axis=0).astype(F32)
    vb = jnp.concatenate([vp_ref[...], vc_ref[...]], axis=0)
    qi = lax.broadcasted_iota(jnp.int32, (blk, 2 * blk), 0)
    sj = lax.broadcasted_iota(jnp.int32, (blk, 2 * blk), 1)
    dist = qi + blk - sj
    valid = (dist >= 0) & (dist < WINDOW) & ((sj >= blk) | (n > 0))
    distf = dist.astype(F32)
    qn = qn_ref[...]
    kn = kn_ref[...]
    scale = ATTN_HEAD_DIM ** -0.5
    for kv in range(ATTN_N_KV):
        k_h = _rms(kf[:, kv * hd:(kv + 1) * hd], kn).astype(BF16)
        v_h = vb[:, kv * hd:(kv + 1) * hd]
        for gq in range(ATTN_Q_PER_KV):
            h = kv * ATTN_Q_PER_KV + gq
            slope = float(2.0 ** (-8.0 * (h + 1) / ATTN_N_HEADS))
            q_h = q_ref[:, h * hd:(h + 1) * hd].astype(F32)
            q_h = (_rms(q_h, qn) * scale).astype(BF16)
            s = _dot_nt(q_h, k_h) - slope * distf
            s = jnp.where(valid, s, NEG)
            sink = sink_ref[h]
            m = jnp.maximum(jnp.max(s, axis=-1, keepdims=True), sink)
            p = jnp.exp(s - m)
            denom = jnp.sum(p, axis=-1, keepdims=True) + jnp.exp(sink - m)
            o = _dot(p.astype(BF16), v_h) / denom
            o_ref[:, h * hd:(h + 1) * hd] = o.astype(BF16)


def _attn(p, sinks, qn_row, kn_row, bsz, t_len):
    blk = ATTN_BLOCK
    nb = t_len // blk
    cur = lambda b, n: b * nb + n
    prev = lambda b, n: b * nb + jnp.maximum(n - 1, 0)
    return pl.pallas_call(
        _attn_kernel,
        grid=(bsz, nb),
        in_specs=[
            pl.BlockSpec(memory_space=pltpu.SMEM),
            pl.BlockSpec((blk, ATTN_D), lambda b, n: (cur(b, n), P_Q // ATTN_D)),
            pl.BlockSpec((blk, ATTN_KV_D), lambda b, n: (cur(b, n), P_K // ATTN_KV_D)),
            pl.BlockSpec((blk, ATTN_KV_D), lambda b, n: (prev(b, n), P_K // ATTN_KV_D)),
            pl.BlockSpec((blk, ATTN_KV_D), lambda b, n: (cur(b, n), P_V // ATTN_KV_D)),
            pl.BlockSpec((blk, ATTN_KV_D), lambda b, n: (prev(b, n), P_V // ATTN_KV_D)),
            pl.BlockSpec((1, ATTN_HEAD_DIM), lambda b, n: (0, 0)),
            pl.BlockSpec((1, ATTN_HEAD_DIM), lambda b, n: (0, 0)),
        ],
        out_specs=pl.BlockSpec((blk, ATTN_D), lambda b, n: (cur(b, n), 0)),
        out_shape=jax.ShapeDtypeStruct((bsz * t_len, ATTN_D), BF16),
        compiler_params=_params(("parallel", "arbitrary")),
        name="swa",
    )(sinks, p, p, p, p, p, qn_row, kn_row)


def _merge_kernel(x_ref, ys_ref, ya_ref, gs_ref, ga_ref, wos_ref, woa_ref, wout_ref, o_ref):
    j = pl.program_id(1)

    @pl.when(j == 0)
    def _():
        o_ref[...] = x_ref[...]

    ms = _dot(ys_ref[...], wos_ref[...])
    ma = _dot(ya_ref[...], woa_ref[...])
    mg = (jax.nn.sigmoid(gs_ref[...].astype(F32)) * ms
          + jax.nn.sigmoid(ga_ref[...].astype(F32)) * ma)
    o_ref[...] += _dot(mg.astype(BF16), wout_ref[...])


def _merge(x, ys, ya, p, wos, woa, wout, tm, tn):
    m, d = x.shape
    return pl.pallas_call(
        _merge_kernel,
        grid=(m // tm, d // tn),
        in_specs=[
            pl.BlockSpec((tm, d), lambda i, j: (i, 0)),
            pl.BlockSpec((tm, SSM_D_INNER), lambda i, j: (i, 0)),
            pl.BlockSpec((tm, ATTN_D), lambda i, j: (i, 0)),
            pl.BlockSpec((tm, tn), lambda i, j: (i, P_GS // tn + j)),
            pl.BlockSpec((tm, tn), lambda i, j: (i, P_GA // tn + j)),
            pl.BlockSpec((SSM_D_INNER, tn), lambda i, j: (0, j)),
            pl.BlockSpec((ATTN_D, tn), lambda i, j: (0, j)),
            pl.BlockSpec((tn, d), lambda i, j: (j, 0)),
        ],
        out_specs=pl.BlockSpec((tm, d), lambda i, j: (i, 0)),
        out_shape=jax.ShapeDtypeStruct((m, d), F32),
        compiler_params=_params(("parallel", "arbitrary")),
        name="merge",
    )(x, ys, ya, p, p, wos, woa, wout)


def _selection_constants():
    e = np.zeros((SSM_N_GROUPS, LANES, SSM_GROUP_WIDTH), np.float32)
    st = np.zeros((SSM_N_GROUPS, SSM_HEADS_PER_GROUP, LANES), np.float32)
    for g in range(SSM_N_GROUPS):
        for h in range(SSM_HEADS_PER_GROUP):
            e[g, g * SSM_HEADS_PER_GROUP + h, h * SSM_HEAD_DIM:(h + 1) * SSM_HEAD_DIM] = 1.0
            st[g, h, g * SSM_HEADS_PER_GROUP + h] = 1.0
    return jnp.asarray(e, BF16), jnp.asarray(st, BF16)


def _pad_lanes(v):
    return jnp.pad(v.astype(F32), (0, LANES - v.shape[0])).reshape(1, LANES)


def kernel(x, ffn1_norm, ffn1_w_gate, ffn1_w_up, ffn1_w_down, mix_norm, w_in, conv_w, conv_b, dt_bias, a_log, d_skip, ssm_norm, q_norm, k_norm, sinks, w_o_ssm, w_o_attn, w_out, ffn2_norm, ffn2_w_gate, ffn2_w_up, ffn2_w_down):
    bsz, t_len, d = x.shape
    m = bsz * t_len
    depth = ffn1_norm.shape[0]
    e01, selt = _selection_constants()
    xf = x.reshape(m, d)
    for l in range(depth):
        wi = w_in[l]
        w_main = jnp.concatenate(
            [wi[:, W_Z:W_DT], wi[:, W_Q:]], axis=1).astype(BF16)
        w_dt = jnp.pad(wi[:, W_DT:W_Q], ((0, 0), (0, LANES - SSM_N_HEADS))).astype(BF16)

        xf = _ffn(xf, ffn1_norm[l].reshape(1, d), ffn1_w_gate[l].astype(BF16),
                  ffn1_w_up[l].astype(BF16), ffn1_w_down[l].astype(BF16), tm=512, tf=512)

        p, dt_raw = _inproj(xf, mix_norm[l].reshape(1, d), w_main, w_dt, tm=512, tn=1536)

        y_ssm = _ssd(
            p, dt_raw, conv_w[l], conv_b[l].reshape(1, -1),
            _pad_lanes(dt_bias[l]), _pad_lanes(a_log[l]),
            jnp.repeat(a_log[l].astype(F32), SSM_HEAD_DIM).reshape(1, SSM_D_INNER),
            jnp.repeat(d_skip[l].astype(F32), SSM_HEAD_DIM).reshape(1, SSM_D_INNER),
            ssm_norm[l].reshape(1, SSM_D_INNER), e01, selt, bsz, t_len)

        y_attn = _attn(p, sinks[l].astype(F32), q_norm[l].reshape(1, ATTN_HEAD_DIM),
                       k_norm[l].reshape(1, ATTN_HEAD_DIM), bsz, t_len)

        xf = _merge(xf, y_ssm, y_attn, p, w_o_ssm[l].astype(BF16), w_o_attn[l].astype(BF16),
                    w_out[l].astype(BF16), tm=512, tn=512)

        xf = _ffn(xf, ffn2_norm[l].reshape(1, d), ffn2_w_gate[l].astype(BF16),
                  ffn2_w_up[l].astype(BF16), ffn2_w_down[l].astype(BF16), tm=512, tf=512)
    return xf.reshape(bsz, t_len, d)
```

```python
import functools

import jax
import jax.numpy as jnp
import numpy as np
from jax import lax
from jax.experimental import pallas as pl
from jax.experimental.pallas import tpu as pltpu

F32 = jnp.float32
BF16 = jnp.bfloat16

D_MODEL = 2048
SSM_D_INNER = 4096
SSM_HEAD_DIM = 64
SSM_N_HEADS = 64
SSM_N_GROUPS = 8
SSM_HEADS_PER_GROUP = SSM_N_HEADS // SSM_N_GROUPS
SSM_GROUP_WIDTH = SSM_D_INNER // SSM_N_GROUPS
SSM_D_STATE = 128
SSM_CONV = 4
SSM_CHUNK = 256
SSM_GN = SSM_N_GROUPS * SSM_D_STATE
ATTN_HEAD_DIM = 64
ATTN_N_HEADS = 32
ATTN_N_KV = 4
ATTN_Q_PER_KV = 8
ATTN_D = ATTN_N_HEADS * ATTN_HEAD_DIM
ATTN_KV_D = ATTN_N_KV * ATTN_HEAD_DIM
WINDOW = 128
ATTN_BLOCK = 128
D_FF = 5632
EPS = 1e-6
NEG = -1e30

LANES = 128
CONV_HALO = 8

P_Z = 0
P_X = P_Z + SSM_D_INNER
P_B = P_X + SSM_D_INNER
P_C = P_B + SSM_GN
P_Q = P_C + SSM_GN
P_K = P_Q + ATTN_D
P_V = P_K + ATTN_KV_D
P_GS = P_V + ATTN_KV_D
P_GA = P_GS + D_MODEL
P_COLS = P_GA + D_MODEL

W_Z = 0
W_XBC = W_Z + SSM_D_INNER
W_DT = W_XBC + SSM_D_INNER + 2 * SSM_GN
W_Q = W_DT + SSM_N_HEADS
W_K = W_Q + ATTN_D
W_V = W_K + ATTN_KV_D
W_GS = W_V + ATTN_KV_D
W_GA = W_GS + D_MODEL

VMEM_LIMIT = 56 * 1024 * 1024


def _params(sem):
    return pltpu.CompilerParams(dimension_semantics=sem, vmem_limit_bytes=VMEM_LIMIT)


def _rms(x, gain):
    return x * lax.rsqrt(jnp.mean(x * x, axis=-1, keepdims=True) + EPS) * gain


def _silu(x):
    return x * jax.nn.sigmoid(x)


def _dot(a, b):
    return jnp.dot(a, b, preferred_element_type=F32)


def _dot_nt(a, b):
    return lax.dot_general(a, b, (((1,), (1,)), ((), ())), preferred_element_type=F32)


def _split3(x):
    hi = x.astype(BF16)
    r1 = x - hi.astype(F32)
    mid = r1.astype(BF16)
    lo = (r1 - mid.astype(F32)).astype(BF16)
    return hi, mid, lo


def _sel_left(m01, x):
    hi, mid, lo = _split3(x)
    return (_dot(m01, lo) + _dot(m01, mid)) + _dot(m01, hi)


def _sel_right(x, m01):
    hi, mid, lo = _split3(x)
    return (_dot(lo, m01) + _dot(mid, m01)) + _dot(hi, m01)


def _ffn_kernel(x_ref, gain_ref, wg_ref, wu_ref, wd_ref, o_ref, h_ref):
    j = pl.program_id(1)

    @pl.when(j == 0)
    def _():
        x = x_ref[...]
        h_ref[...] = _rms(x, gain_ref[...]).astype(BF16)
        o_ref[...] = x

    h = h_ref[...]
    g = _dot(h, wg_ref[...])
    u = _dot(h, wu_ref[...])
    a = (0.5 * _silu(g) * u).astype(BF16)
    o_ref[...] += _dot(a, wd_ref[...])


def _ffn(x, gain, wg, wu, wd, tm, tf):
    m, d = x.shape
    dff = wg.shape[1]
    return pl.pallas_call(
        _ffn_kernel,
        grid=(m // tm, dff // tf),
        in_specs=[
            pl.BlockSpec((tm, d), lambda i, j: (i, 0)),
            pl.BlockSpec((1, d), lambda i, j: (0, 0)),
            pl.BlockSpec((d, tf), lambda i, j: (0, j)),
            pl.BlockSpec((d, tf), lambda i, j: (0, j)),
            pl.BlockSpec((tf, d), lambda i, j: (j, 0)),
        ],
        out_specs=pl.BlockSpec((tm, d), lambda i, j: (i, 0)),
        out_shape=jax.ShapeDtypeStruct((m, d), F32),
        scratch_shapes=[pltpu.VMEM((tm, d), BF16)],
        compiler_params=_params(("parallel", "arbitrary")),
        name="ffn",
    )(x, gain, wg, wu, wd)


def _inproj_kernel(x_ref, gain_ref, w_ref, wdt_ref, p_ref, dt_ref, h_ref):
    j = pl.program_id(1)

    @pl.when(j == 0)
    def _():
        h = _rms(x_ref[...], gain_ref[...]).astype(BF16)
        h_ref[...] = h
        dt_ref[...] = _dot(h, wdt_ref[...])

    p_ref[...] = _dot(h_ref[...], w_ref[...]).astype(BF16)


def _inproj(x, gain, w, wdt, tm, tn):
    m, d = x.shape
    n = w.shape[1]
    return pl.pallas_call(
        _inproj_kernel,
        grid=(m // tm, n // tn),
        in_specs=[
            pl.BlockSpec((tm, d), lambda i, j: (i, 0)),
            pl.BlockSpec((1, d), lambda i, j: (0, 0)),
            pl.BlockSpec((d, tn), lambda i, j: (0, j)),
            pl.BlockSpec((d, LANES), lambda i, j: (0, 0)),
        ],
        out_specs=[
            pl.BlockSpec((tm, tn), lambda i, j: (i, j)),
            pl.BlockSpec((tm, LANES), lambda i, j: (i, 0)),
        ],
        out_shape=[
            jax.ShapeDtypeStruct((m, n), BF16),
            jax.ShapeDtypeStruct((m, LANES), F32),
        ],
        scratch_shapes=[pltpu.VMEM((tm, d), BF16)],
        compiler_params=_params(("parallel", "arbitrary")),
        name="in_proj",
    )(x, gain, w, wdt)


def _ssd_kernel(z_ref, x_ref, b_ref, c_ref, dt_ref,
                cwx_ref, cwb_ref, cwc_ref, cbx_ref, cbb_ref, cbc_ref,
                dtb_ref, alog_ref, dsk_ref, nrm_ref, e_ref, selt_ref,
                o_ref, ext_ref, state_ref):
    c = pl.program_id(2)
    L = SSM_CHUNK
    GW = SSM_GROUP_WIDTH
    N = SSM_D_STATE
    XBC = GW + 2 * N

    @pl.when(c == 0)
    def _():
        ext_ref[0:CONV_HALO, :] = jnp.zeros((CONV_HALO, XBC), F32)
        state_ref[...] = jnp.zeros_like(state_ref)

    ext_ref[CONV_HALO:CONV_HALO + L, 0:GW] = x_ref[...].astype(F32)
    ext_ref[CONV_HALO:CONV_HALO + L, GW:GW + N] = b_ref[...].astype(F32)
    ext_ref[CONV_HALO:CONV_HALO + L, GW + N:XBC] = c_ref[...].astype(F32)

    cw = jnp.concatenate([cwx_ref[...], cwb_ref[...], cwc_ref[...]], axis=1)
    cbias = jnp.concatenate([cbx_ref[...], cbb_ref[...], cbc_ref[...]], axis=1)
    acc = cbias + cw[0:1, :] * ext_ref[CONV_HALO - 3:CONV_HALO - 3 + L, :]
    for k in range(1, SSM_CONV):
        off = CONV_HALO - (SSM_CONV - 1) + k
        acc = acc + cw[k:k + 1, :] * ext_ref[off:off + L, :]
    ext_ref[0:CONV_HALO, :] = ext_ref[L:L + CONV_HALO, :]
    xbc = _silu(acc)
    xs = xbc[:, 0:GW]
    bm = xbc[:, GW:GW + N]
    cm = xbc[:, GW + N:XBC]

    dtr = dt_ref[...] + dtb_ref[...]
    dt = jnp.maximum(dtr, 0.0) + jnp.log1p(jnp.exp(-jnp.abs(dtr)))
    e01 = e_ref[...]
    dt_x = _sel_right(dt, e01)
    adt_c = dt * (-jnp.exp(alog_ref[...]))
    row = lax.broadcasted_iota(jnp.int32, (L, L), 0)
    col = lax.broadcasted_iota(jnp.int32, (L, L), 1)
    tril = row >= col
    tril01 = tril.astype(BF16)
    acum_c = _sel_left(tril01, adt_c)
    selt = selt_ref[...]
    hi, mid, lo = _split3(acum_c)
    acum_r = (_dot_nt(selt, lo) + _dot_nt(selt, mid)) + _dot_nt(selt, hi)
    acum_x = (_dot(lo, e01) + _dot(mid, e01)) + _dot(hi, e01)

    bt = bm.T.astype(BF16)
    cmb = cm.astype(BF16)
    cbm = _dot(cmb, bt)
    xdt = xs * dt_x
    xdt_b = xdt.astype(BF16)

    lane = lax.broadcasted_iota(jnp.int32, (L, LANES), 1)
    low_half = lane < SSM_HEAD_DIM
    pairs = []
    for pair in range(SSM_HEADS_PER_GROUP // 2):
        xp = xdt_b[:, pair * LANES:(pair + 1) * LANES]
        y_pair = None
        for half in range(2):
            h = 2 * pair + half
            seg = acum_x[:, h * SSM_HEAD_DIM:h * SSM_HEAD_DIM + 1] - acum_r[h:h + 1, :]
            w = cbm * jnp.exp(jnp.where(tril, seg, NEG))
            keep = low_half if half == 0 else jnp.logical_not(low_half)
            xh = jnp.where(keep, xp, jnp.zeros_like(xp))
            yh = _dot(w.astype(BF16), xh)
            y_pair = yh if y_pair is None else y_pair + yh
        pairs.append(y_pair)
    y_diag = jnp.concatenate(pairs, axis=1)

    state = state_ref[...]
    y_off = jnp.exp(acum_x) * _dot(cmb, state.astype(BF16))
    y = y_diag + y_off + dsk_ref[...] * xs

    last = acum_x[L - 1:L, :]
    xdec = (xdt * jnp.exp(last - acum_x)).astype(BF16)
    state_ref[...] = state * jnp.exp(last) + _dot(bt, xdec)

    yg = y * _silu(z_ref[...].astype(F32))
    o_ref[...] = _rms(yg, nrm_ref[...]).astype(BF16)


def _ssd(p, dt_raw, conv_w, conv_b, dtb_row, alog_row, dskip_x, nrm_row, e01, selt, bsz, t_len):
    L = SSM_CHUNK
    nc = t_len // L
    GW = SSM_GROUP_WIDTH
    N = SSM_D_STATE
    rowblk = lambda b, g, c: b * nc + c
    return pl.pallas_call(
        _ssd_kernel,
        grid=(bsz, SSM_N_GROUPS, nc),
        in_specs=[
            pl.BlockSpec((L, GW), lambda b, g, c: (rowblk(b, g, c), P_Z // GW + g)),
            pl.BlockSpec((L, GW), lambda b, g, c: (rowblk(b, g, c), P_X // GW + g)),
            pl.BlockSpec((L, N), lambda b, g, c: (rowblk(b, g, c), P_B // N + g)),
            pl.BlockSpec((L, N), lambda b, g, c: (rowblk(b, g, c), P_C // N + g)),
            pl.BlockSpec((L, LANES), lambda b, g, c: (rowblk(b, g, c), 0)),
            pl.BlockSpec((SSM_CONV, GW), lambda b, g, c: (0, g)),
            pl.BlockSpec((SSM_CONV, N), lambda b, g, c: (0, SSM_D_INNER // N + g)),
            pl.BlockSpec((SSM_CONV, N), lambda b, g, c: (0, (SSM_D_INNER + SSM_GN) // N + g)),
            pl.BlockSpec((1, GW), lambda b, g, c: (0, g)),
            pl.BlockSpec((1, N), lambda b, g, c: (0, SSM_D_INNER // N + g)),
            pl.BlockSpec((1, N), lambda b, g, c: (0, (SSM_D_INNER + SSM_GN) // N + g)),
            pl.BlockSpec((1, LANES), lambda b, g, c: (0, 0)),
            pl.BlockSpec((1, LANES), lambda b, g, c: (0, 0)),
            pl.BlockSpec((1, GW), lambda b, g, c: (0, g)),
            pl.BlockSpec((1, GW), lambda b, g, c: (0, g)),
            pl.BlockSpec((None, LANES, GW), lambda b, g, c: (g, 0, 0)),
            pl.BlockSpec((None, SSM_HEADS_PER_GROUP, LANES), lambda b, g, c: (g, 0, 0)),
        ],
        out_specs=pl.BlockSpec((L, GW), lambda b, g, c: (rowblk(b, g, c), g)),
        out_shape=jax.ShapeDtypeStruct((bsz * t_len, SSM_D_INNER), BF16),
        scratch_shapes=[
            pltpu.VMEM((CONV_HALO + L, GW + 2 * N), F32),
            pltpu.VMEM((N, GW), F32),
        ],
        compiler_params=_params(("parallel", "parallel", "arbitrary")),
        name="ssd",
    )(p, p, p, p, dt_raw, conv_w, conv_w, conv_w, conv_b, conv_b, conv_b,
      dtb_row, alog_row, dskip_x, nrm_row, e01, selt)


def _half_rms_scale(x, low):
    sq = x * x
    zero = jnp.zeros_like(sq)
    ss_lo = jnp.sum(jnp.where(low, sq, zero), axis=-1, keepdims=True)
    ss_hi = jnp.sum(jnp.where(low, zero, sq), axis=-1, keepdims=True)
    inv = 1.0 / ATTN_HEAD_DIM
    return jnp.where(low, lax.rsqrt(ss_lo * inv + EPS), lax.rsqrt(ss_hi * inv + EPS))


def _attn_kernel(sink_ref, slope_ref, q_ref, kc_ref, kp_ref, vc_ref, vp_ref, qn_ref, kn_ref, o_ref):
    n = pl.program_id(1)
    blk = ATTN_BLOCK
    hd = ATTN_HEAD_DIM
    kf = jnp.concatenate([kp_ref[...], kc_ref[...]], axis=0).astype(F32)
    vf = jnp.concatenate([vp_ref[...], vc_ref[...]], axis=0).astype(F32)
    sj = lax.broadcasted_iota(jnp.int32, (2 * blk, blk), 0)
    qi = lax.broadcasted_iota(jnp.int32, (2 * blk, blk), 1)
    dist = qi + blk - sj
    valid = (dist >= 0) & (dist < WINDOW) & ((sj >= blk) | (n > 0))
    ndm = jnp.where(valid, -dist.astype(F32), NEG)
    npair = ATTN_Q_PER_KV // 2
    ndm4 = jnp.concatenate([ndm] * npair, axis=1)
    low_k = lax.broadcasted_iota(jnp.int32, (2 * blk, LANES), 1) < hd
    low_q = lax.broadcasted_iota(jnp.int32, (npair * blk, LANES), 1) < hd
    qn = qn_ref[...] * (ATTN_HEAD_DIM ** -0.5)
    kn = kn_ref[...]
    zk = jnp.zeros((2 * blk, LANES), F32)
    for kvp in range(ATTN_N_KV // 2):
        k2 = kf[:, kvp * LANES:(kvp + 1) * LANES]
        k2 = k2 * _half_rms_scale(k2, low_k) * kn
        v2 = vf[:, kvp * LANES:(kvp + 1) * LANES]
        k2r = pltpu.roll(k2, hd, axis=1)
        v2r = pltpu.roll(v2, hd, axis=1)
        for half in range(2):
            kv = 2 * kvp + half
            k_lo_src, k_hi_src = (k2, k2r) if half == 0 else (k2r, k2)
            v_lo_src, v_hi_src = (v2, v2r) if half == 0 else (v2r, v2)
            k_lo = jnp.where(low_k, k_lo_src, zk).astype(BF16)
            k_hi = jnp.where(low_k, zk, k_hi_src).astype(BF16)
            v_lo = jnp.where(low_k, v_lo_src, zk).astype(BF16)
            v_hi = jnp.where(low_k, zk, v_hi_src).astype(BF16)
            q4 = jnp.concatenate(
                [q_ref[:, (kv * npair + j) * LANES:(kv * npair + j + 1) * LANES] for j in range(npair)],
                axis=0).astype(F32)
            qb = (q4 * _half_rms_scale(q4, low_q) * qn).astype(BF16)
            o4 = None
            for e in range(2):
                u = 2 * kv + e
                st = _dot_nt(k_lo if e == 0 else k_hi, qb) + slope_ref[u:u + 1, :] * ndm4
                sink = sink_ref[u:u + 1, :]
                m = jnp.maximum(jnp.max(st, axis=0, keepdims=True), sink)
                p = jnp.exp(st - m)
                denom = jnp.sum(p, axis=0, keepdims=True) + jnp.exp(sink - m)
                pn = (p * (1.0 / denom)).astype(BF16)
                oe = lax.dot_general(pn, v_lo if e == 0 else v_hi, (((0,), (0,)), ((), ())),
                                     preferred_element_type=F32)
                o4 = oe if o4 is None else o4 + oe
            for j in range(npair):
                hp = kv * npair + j
                o_ref[:, hp * LANES:(hp + 1) * LANES] = o4[j * blk:(j + 1) * blk, :].astype(BF16)


def _head_table(per_head):
    npair = ATTN_Q_PER_KV // 2
    t = per_head.astype(F32).reshape(ATTN_N_KV, npair, 2).transpose(0, 2, 1)
    return jnp.repeat(t.reshape(2 * ATTN_N_KV, npair), LANES, axis=1)


def _attn(p, sinks, qn_row, kn_row, bsz, t_len):
    blk = ATTN_BLOCK
    slopes = jnp.asarray(
        [2.0 ** (-8.0 * (h + 1) / ATTN_N_HEADS) for h in range(ATTN_N_HEADS)], F32)
    nb = t_len // blk
    cur = lambda b, n: b * nb + n
    prev = lambda b, n: b * nb + jnp.maximum(n - 1, 0)
    return pl.pallas_call(
        _attn_kernel,
        grid=(bsz, nb),
        in_specs=[
            pl.BlockSpec((2 * ATTN_N_KV, 4 * LANES), lambda b, n: (0, 0)),
            pl.BlockSpec((2 * ATTN_N_KV, 4 * LANES), lambda b, n: (0, 0)),
            pl.BlockSpec((blk, ATTN_D), lambda b, n: (cur(b, n), P_Q // ATTN_D)),
            pl.BlockSpec((blk, ATTN_KV_D), lambda b, n: (cur(b, n), P_K // ATTN_KV_D)),
            pl.BlockSpec((blk, ATTN_KV_D), lambda b, n: (prev(b, n), P_K // ATTN_KV_D)),
            pl.BlockSpec((blk, ATTN_KV_D), lambda b, n: (cur(b, n), P_V // ATTN_KV_D)),
            pl.BlockSpec((blk, ATTN_KV_D), lambda b, n: (prev(b, n), P_V // ATTN_KV_D)),
            pl.BlockSpec((1, LANES), lambda b, n: (0, 0)),
            pl.BlockSpec((1, LANES), lambda b, n: (0, 0)),
        ],
        out_specs=pl.BlockSpec((blk, ATTN_D), lambda b, n: (cur(b, n), 0)),
        out_shape=jax.ShapeDtypeStruct((bsz * t_len, ATTN_D), BF16),
        compiler_params=_params(("parallel", "arbitrary")),
        name="swa",
    )(_head_table(sinks), _head_table(slopes), p, p, p, p, p, qn_row, kn_row)


def _merge_kernel(x_ref, ys_ref, ya_ref, gs_ref, ga_ref, wos_ref, woa_ref, wout_ref, o_ref):
    j = pl.program_id(1)

    @pl.when(j == 0)
    def _():
        o_ref[...] = x_ref[...]

    ms = _dot(ys_ref[...], wos_ref[...])
    ma = _dot(ya_ref[...], woa_ref[...])
    mg = (jax.nn.sigmoid(gs_ref[...].astype(F32)) * ms
          + jax.nn.sigmoid(ga_ref[...].astype(F32)) * ma)
    o_ref[...] += _dot(mg.astype(BF16), wout_ref[...])


def _merge(x, ys, ya, p, wos, woa, wout, tm, tn):
    m, d = x.shape
    return pl.pallas_call(
        _merge_kernel,
        grid=(m // tm, d // tn),
        in_specs=[
            pl.BlockSpec((tm, d), lambda i, j: (i, 0)),
            pl.BlockSpec((tm, SSM_D_INNER), lambda i, j: (i, 0)),
            pl.BlockSpec((tm, ATTN_D), lambda i, j: (i, 0)),
            pl.BlockSpec((tm, tn), lambda i, j: (i, P_GS // tn + j)),
            pl.BlockSpec((tm, tn), lambda i, j: (i, P_GA // tn + j)),
            pl.BlockSpec((SSM_D_INNER, tn), lambda i, j: (0, j)),
            pl.BlockSpec((ATTN_D, tn), lambda i, j: (0, j)),
            pl.BlockSpec((tn, d), lambda i, j: (j, 0)),
        ],
        out_specs=pl.BlockSpec((tm, d), lambda i, j: (i, 0)),
        out_shape=jax.ShapeDtypeStruct((m, d), F32),
        compiler_params=_params(("parallel", "arbitrary")),
        name="merge",
    )(x, ys, ya, p, p, wos, woa, wout)


def _selection_constants():
    e = np.zeros((SSM_N_GROUPS, LANES, SSM_GROUP_WIDTH), np.float32)
    st = np.zeros((SSM_N_GROUPS, SSM_HEADS_PER_GROUP, LANES), np.float32)
    for g in range(SSM_N_GROUPS):
        for h in range(SSM_HEADS_PER_GROUP):
            e[g, g * SSM_HEADS_PER_GROUP + h, h * SSM_HEAD_DIM:(h + 1) * SSM_HEAD_DIM] = 1.0
            st[g, h, g * SSM_HEADS_PER_GROUP + h] = 1.0
    return jnp.asarray(e, BF16), jnp.asarray(st, BF16)


def _pad_lanes(v):
    return jnp.pad(v.astype(F32), (0, LANES - v.shape[0])).reshape(1, LANES)


def kernel(x, ffn1_norm, ffn1_w_gate, ffn1_w_up, ffn1_w_down, mix_norm, w_in, conv_w, conv_b, dt_bias, a_log, d_skip, ssm_norm, q_norm, k_norm, sinks, w_o_ssm, w_o_attn, w_out, ffn2_norm, ffn2_w_gate, ffn2_w_up, ffn2_w_down):
    bsz, t_len, d = x.shape
    m = bsz * t_len
    depth = ffn1_norm.shape[0]
    e01, selt = _selection_constants()
    xf = x.reshape(m, d)
    for l in range(depth):
        wi = w_in[l]
        w_main = jnp.concatenate(
            [wi[:, W_Z:W_DT], wi[:, W_Q:]], axis=1).astype(BF16)
        w_dt = jnp.pad(wi[:, W_DT:W_Q], ((0, 0), (0, LANES - SSM_N_HEADS))).astype(BF16)

        xf = _ffn(xf, ffn1_norm[l].reshape(1, d), ffn1_w_gate[l].astype(BF16),
                  ffn1_w_up[l].astype(BF16), ffn1_w_down[l].astype(BF16), tm=1024, tf=512)

        p, dt_raw = _inproj(xf, mix_norm[l].reshape(1, d), w_main, w_dt, tm=1024, tn=1536)

        y_ssm = _ssd(
            p, dt_raw, conv_w[l], conv_b[l].reshape(1, -1),
            _pad_lanes(dt_bias[l]), _pad_lanes(a_log[l]),
            jnp.repeat(d_skip[l].astype(F32), SSM_HEAD_DIM).reshape(1, SSM_D_INNER),
            ssm_norm[l].reshape(1, SSM_D_INNER), e01, selt, bsz, t_len)

        y_attn = _attn(p, sinks[l].astype(F32),
                       jnp.tile(q_norm[l].astype(F32), LANES // ATTN_HEAD_DIM).reshape(1, LANES),
                       jnp.tile(k_norm[l].astype(F32), LANES // ATTN_HEAD_DIM).reshape(1, LANES),
                       bsz, t_len)

        xf = _merge(xf, y_ssm, y_attn, p, w_o_ssm[l].astype(BF16), w_o_attn[l].astype(BF16),
                    w_out[l].astype(BF16), tm=512, tn=512)

        xf = _ffn(xf, ffn2_norm[l].reshape(1, d), ffn2_w_gate[l].astype(BF16),
                  ffn2_w_up[l].astype(BF16), ffn2_w_down[l].astype(BF16), tm=1024, tf=512)
    return xf.reshape(bsz, t_len, d)
```

```python
import functools

import jax
import jax.numpy as jnp
import numpy as np
from jax import lax
from jax.experimental import pallas as pl
from jax.experimental.pallas import tpu as pltpu

F32 = jnp.float32
BF16 = jnp.bfloat16

D_MODEL = 2048
SSM_D_INNER = 4096
SSM_HEAD_DIM = 64
SSM_N_HEADS = 64
SSM_N_GROUPS = 8
SSM_HEADS_PER_GROUP = SSM_N_HEADS // SSM_N_GROUPS
SSM_GROUP_WIDTH = SSM_D_INNER // SSM_N_GROUPS
SSM_D_STATE = 128
SSM_CONV = 4
SSM_CHUNK = 256
SSM_GN = SSM_N_GROUPS * SSM_D_STATE
ATTN_HEAD_DIM = 64
ATTN_N_HEADS = 32
ATTN_N_KV = 4
ATTN_Q_PER_KV = 8
ATTN_D = ATTN_N_HEADS * ATTN_HEAD_DIM
ATTN_KV_D = ATTN_N_KV * ATTN_HEAD_DIM
WINDOW = 128
ATTN_BLOCK = 128
D_FF = 5632
EPS = 1e-6
NEG = -1e30
LOG2E = 1.4426950408889634

LANES = 128
CONV_HALO = 8

P_Z = 0
P_X = P_Z + SSM_D_INNER
P_B = P_X + SSM_D_INNER
P_C = P_B + SSM_GN
P_Q = P_C + SSM_GN
P_K = P_Q + ATTN_D
P_V = P_K + ATTN_KV_D
P_GS = P_V + ATTN_KV_D
P_GA = P_GS + D_MODEL
P_COLS = P_GA + D_MODEL

W_Z = 0
W_XBC = W_Z + SSM_D_INNER
W_DT = W_XBC + SSM_D_INNER + 2 * SSM_GN
W_Q = W_DT + SSM_N_HEADS
W_K = W_Q + ATTN_D
W_V = W_K + ATTN_KV_D
W_GS = W_V + ATTN_KV_D
W_GA = W_GS + D_MODEL

VMEM_LIMIT = 56 * 1024 * 1024


def _params(sem):
    return pltpu.CompilerParams(dimension_semantics=sem, vmem_limit_bytes=VMEM_LIMIT)


def _rms(x, gain):
    return x * lax.rsqrt(jnp.mean(x * x, axis=-1, keepdims=True) + EPS) * gain


def _silu(x):
    return x * jax.nn.sigmoid(x)


def _dot(a, b):
    return jnp.dot(a, b, preferred_element_type=F32)


def _dot_nt(a, b):
    return lax.dot_general(a, b, (((1,), (1,)), ((), ())), preferred_element_type=F32)


def _split3(x):
    hi = x.astype(BF16)
    r1 = x - hi.astype(F32)
    mid = r1.astype(BF16)
    lo = (r1 - mid.astype(F32)).astype(BF16)
    return hi, mid, lo


def _sel_left(m01, x):
    hi, mid, lo = _split3(x)
    return (_dot(m01, lo) + _dot(m01, mid)) + _dot(m01, hi)


def _sel_right(x, m01):
    hi, mid, lo = _split3(x)
    return (_dot(lo, m01) + _dot(mid, m01)) + _dot(hi, m01)


PACK_TN = 512


def _pack_kernel(a_ref, b_ref, dtw_ref, w_ref, wdt_ref):
    j = pl.program_id(0)
    tn = PACK_TN
    shift = SSM_N_HEADS

    @pl.when(j < P_Q // tn)
    def _():
        w_ref[...] = a_ref[...].astype(BF16)

    @pl.when(j >= P_Q // tn)
    def _():
        ab = jnp.concatenate([a_ref[...], b_ref[...]], axis=1)
        w_ref[...] = ab[:, shift:shift + tn].astype(BF16)

    @pl.when(j == 0)
    def _():
        lane = lax.broadcasted_iota(jnp.int32, dtw_ref.shape, 1)
        wdt_ref[...] = jnp.where(lane < shift, dtw_ref[...], 0.0).astype(BF16)


def _pack_w_in(wi):
    d, n_in = wi.shape
    tn = PACK_TN
    nb = tn // LANES
    return pl.pallas_call(
        _pack_kernel,
        grid=(P_COLS // tn,),
        in_specs=[
            pl.BlockSpec((d, tn), lambda j: (0, j)),
            pl.BlockSpec((d, LANES), lambda j: (0, (j + 1) * nb)),
            pl.BlockSpec((d, LANES), lambda j: (0, W_DT // LANES)),
        ],
        out_specs=[
            pl.BlockSpec((d, tn), lambda j: (0, j)),
            pl.BlockSpec((d, LANES), lambda j: (0, 0)),
        ],
        out_shape=[
            jax.ShapeDtypeStruct((d, P_COLS), BF16),
            jax.ShapeDtypeStruct((d, LANES), BF16),
        ],
        compiler_params=_params(("arbitrary",)),
        name="pack_w_in",
    )(wi, wi, wi)


def _ffn_kernel(x_ref, gain_ref, wg_ref, wu_ref, wd_ref, o_ref, h_ref):
    j = pl.program_id(1)

    @pl.when(j == 0)
    def _():
        x = x_ref[...]
        h_ref[...] = _rms(x, gain_ref[...]).astype(BF16)
        o_ref[...] = x

    h = h_ref[...]
    g = _dot(h, wg_ref[...])
    u = _dot(h, wu_ref[...])
    a = (0.5 * _silu(g) * u).astype(BF16)
    o_ref[...] += _dot(a, wd_ref[...])


def _ffn(x, gain, wg, wu, wd, tm, tf):
    m, d = x.shape
    dff = wg.shape[1]
    return pl.pallas_call(
        _ffn_kernel,
        grid=(m // tm, dff // tf),
        in_specs=[
            pl.BlockSpec((tm, d), lambda i, j: (i, 0)),
            pl.BlockSpec((1, d), lambda i, j: (0, 0)),
            pl.BlockSpec((d, tf), lambda i, j: (0, j)),
            pl.BlockSpec((d, tf), lambda i, j: (0, j)),
            pl.BlockSpec((tf, d), lambda i, j: (j, 0)),
        ],
        out_specs=pl.BlockSpec((tm, d), lambda i, j: (i, 0)),
        out_shape=jax.ShapeDtypeStruct((m, d), F32),
        scratch_shapes=[pltpu.VMEM((tm, d), BF16)],
        compiler_params=_params(("parallel", "arbitrary")),
        name="ffn",
    )(x, gain, wg, wu, wd)


def _inproj_kernel(x_ref, gain_ref, w_ref, wdt_ref, p_ref, dt_ref, h_ref):
    j = pl.program_id(1)

    @pl.when(j == 0)
    def _():
        h = _rms(x_ref[...], gain_ref[...]).astype(BF16)
        h_ref[...] = h
        dt_ref[...] = _dot(h, wdt_ref[...])

    p_ref[...] = _dot(h_ref[...], w_ref[...]).astype(BF16)


def _inproj(x, gain, w, wdt, tm, tn):
    m, d = x.shape
    n = w.shape[1]
    return pl.pallas_call(
        _inproj_kernel,
        grid=(m // tm, n // tn),
        in_specs=[
            pl.BlockSpec((tm, d), lambda i, j: (i, 0)),
            pl.BlockSpec((1, d), lambda i, j: (0, 0)),
            pl.BlockSpec((d, tn), lambda i, j: (0, j)),
            pl.BlockSpec((d, LANES), lambda i, j: (0, 0)),
        ],
        out_specs=[
            pl.BlockSpec((tm, tn), lambda i, j: (i, j)),
            pl.BlockSpec((tm, LANES), lambda i, j: (i, 0)),
        ],
        out_shape=[
            jax.ShapeDtypeStruct((m, n), BF16),
            jax.ShapeDtypeStruct((m, LANES), F32),
        ],
        scratch_shapes=[pltpu.VMEM((tm, d), BF16)],
        compiler_params=_params(("parallel", "arbitrary")),
        name="in_proj",
    )(x, gain, w, wdt)


def _ssd_kernel(z_ref, x_ref, b_ref, c_ref, dt_ref,
                cwx_ref, cwb_ref, cwc_ref, cbx_ref, cbb_ref, cbc_ref,
                dtb_ref, alog_ref, dsk_ref, nrm_ref, e_ref, selt_ref,
                o_ref, ext_ref, state_ref):
    c = pl.program_id(2)
    L = SSM_CHUNK
    GW = SSM_GROUP_WIDTH
    N = SSM_D_STATE
    XBC = GW + 2 * N

    @pl.when(c == 0)
    def _():
        ext_ref[0:CONV_HALO, :] = jnp.zeros((CONV_HALO, XBC), F32)
        state_ref[...] = jnp.zeros_like(state_ref)

    ext_ref[CONV_HALO:CONV_HALO + L, 0:GW] = x_ref[...].astype(F32)
    ext_ref[CONV_HALO:CONV_HALO + L, GW:GW + N] = b_ref[...].astype(F32)
    ext_ref[CONV_HALO:CONV_HALO + L, GW + N:XBC] = c_ref[...].astype(F32)

    cw = jnp.concatenate([cwx_ref[...], cwb_ref[...], cwc_ref[...]], axis=1)
    cbias = jnp.concatenate([cbx_ref[...], cbb_ref[...], cbc_ref[...]], axis=1)
    ext = ext_ref[...]
    acc = cw[0:1, :] * ext
    for k in range(1, SSM_CONV):
        acc = pltpu.roll(acc, 1, axis=0) + cw[k:k + 1, :] * ext
    ext_ref[0:CONV_HALO, :] = ext_ref[L:L + CONV_HALO, :]
    xbc = _silu(acc[CONV_HALO:, :] + cbias)
    xs = xbc[:, 0:GW]
    bm = xbc[:, GW:GW + N]
    cm = xbc[:, GW + N:XBC]

    dtr = dt_ref[...] + dtb_ref[...]
    dt = jnp.maximum(dtr, 0.0) + jnp.log1p(jnp.exp(-jnp.abs(dtr)))
    e01 = e_ref[...]
    dt_x = _sel_right(dt, e01)
    adt_c = dt * (-LOG2E * jnp.exp(alog_ref[...]))
    row = lax.broadcasted_iota(jnp.int32, (L, L), 0)
    col = lax.broadcasted_iota(jnp.int32, (L, L), 1)
    tril = row >= col
    tril01 = tril.astype(BF16)
    acum_c = _sel_left(tril01, adt_c)
    selt = selt_ref[...]
    hi, mid, lo = _split3(acum_c)
    acum_r = (_dot_nt(selt, lo) + _dot_nt(selt, mid)) + _dot_nt(selt, hi)
    acum_x = (_dot(lo, e01) + _dot(mid, e01)) + _dot(hi, e01)

    bt = bm.T.astype(BF16)
    cmb = cm.astype(BF16)
    H = L // 2
    cb_top = _dot(cmb[0:H, :], bt[:, 0:H])
    cb_bot = _dot(cmb[H:L, :], bt)
    xdt = xs * dt_x
    xdt_b = xdt.astype(BF16)

    tri = tril[0:H, 0:H]
    lane = lax.broadcasted_iota(jnp.int32, (L, LANES), 1)
    low_half = lane < SSM_HEAD_DIM
    pairs = []
    for pair in range(SSM_HEADS_PER_GROUP // 2):
        xp = xdt_b[:, pair * LANES:(pair + 1) * LANES]
        y_top = None
        y_bot = None
        for half in range(2):
            h = 2 * pair + half
            a_col = acum_x[:, h * SSM_HEAD_DIM:h * SSM_HEAD_DIM + 1]
            a_row = acum_r[h:h + 1, :]
            w00 = cb_top * jnp.exp2(jnp.where(tri, a_col[0:H] - a_row[:, 0:H], NEG))
            w10 = cb_bot[:, 0:H] * jnp.exp2(a_col[H:L] - a_row[:, 0:H])
            w11 = cb_bot[:, H:L] * jnp.exp2(jnp.where(tri, a_col[H:L] - a_row[:, H:L], NEG))
            keep = low_half if half == 0 else jnp.logical_not(low_half)
            xh = jnp.where(keep, xp, jnp.zeros_like(xp))
            yt = _dot(w00.astype(BF16), xh[0:H, :])
            yb = _dot(jnp.concatenate([w10, w11], axis=1).astype(BF16), xh)
            y_top = yt if y_top is None else y_top + yt
            y_bot = yb if y_bot is None else y_bot + yb
        pairs.append(jnp.concatenate([y_top, y_bot], axis=0))
    y_diag = jnp.concatenate(pairs, axis=1)

    state = state_ref[...]
    y_off = jnp.exp2(acum_x) * _dot(cmb, state.astype(BF16))
    y = y_diag + y_off + dsk_ref[...] * xs

    last = acum_x[L - 1:L, :]
    xdec = (xdt * jnp.exp2(last - acum_x)).astype(BF16)
    state_ref[...] = state * jnp.exp2(last) + _dot(bt, xdec)

    yg = y * _silu(z_ref[...].astype(F32))
    o_ref[...] = _rms(yg, nrm_ref[...]).astype(BF16)


def _ssd(p, dt_raw, conv_w, conv_b, dtb_row, alog_row, dskip_x, nrm_row, e01, selt, bsz, t_len):
    L = SSM_CHUNK
    nc = t_len // L
    GW = SSM_GROUP_WIDTH
    N = SSM_D_STATE
    rowblk = lambda b, g, c: b * nc + c
    return pl.pallas_call(
        _ssd_kernel,
        grid=(bsz, SSM_N_GROUPS, nc),
        in_specs=[
            pl.BlockSpec((L, GW), lambda b, g, c: (rowblk(b, g, c), P_Z // GW + g)),
            pl.BlockSpec((L, GW), lambda b, g, c: (rowblk(b, g, c), P_X // GW + g)),
            pl.BlockSpec((L, N), lambda b, g, c: (rowblk(b, g, c), P_B // N + g)),
            pl.BlockSpec((L, N), lambda b, g, c: (rowblk(b, g, c), P_C // N + g)),
            pl.BlockSpec((L, LANES), lambda b, g, c: (rowblk(b, g, c), 0)),
            pl.BlockSpec((SSM_CONV, GW), lambda b, g, c: (0, g)),
            pl.BlockSpec((SSM_CONV, N), lambda b, g, c: (0, SSM_D_INNER // N + g)),
            pl.BlockSpec((SSM_CONV, N), lambda b, g, c: (0, (SSM_D_INNER + SSM_GN) // N + g)),
            pl.BlockSpec((1, GW), lambda b, g, c: (0, g)),
            pl.BlockSpec((1, N), lambda b, g, c: (0, SSM_D_INNER // N + g)),
            pl.BlockSpec((1, N), lambda b, g, c: (0, (SSM_D_INNER + SSM_GN) // N + g)),
            pl.BlockSpec((1, LANES), lambda b, g, c: (0, 0)),
            pl.BlockSpec((1, LANES), lambda b, g, c: (0, 0)),
            pl.BlockSpec((1, GW), lambda b, g, c: (0, g)),
            pl.BlockSpec((1, GW), lambda b, g, c: (0, g)),
            pl.BlockSpec((None, LANES, GW), lambda b, g, c: (g, 0, 0)),
            pl.BlockSpec((None, SSM_HEADS_PER_GROUP, LANES), lambda b, g, c: (g, 0, 0)),
        ],
        out_specs=pl.BlockSpec((L, GW), lambda b, g, c: (rowblk(b, g, c), g)),
        out_shape=jax.ShapeDtypeStruct((bsz * t_len, SSM_D_INNER), BF16),
        scratch_shapes=[
            pltpu.VMEM((CONV_HALO + L, GW + 2 * N), F32),
            pltpu.VMEM((N, GW), F32),
        ],
        compiler_params=_params(("parallel", "parallel", "arbitrary")),
        name="ssd",
    )(p, p, p, p, dt_raw, conv_w, conv_w, conv_w, conv_b, conv_b, conv_b,
      dtb_row, alog_row, dskip_x, nrm_row, e01, selt)


def _half_rms_scale(x, low):
    sq = x * x
    zero = jnp.zeros_like(sq)
    ss_lo = jnp.sum(jnp.where(low, sq, zero), axis=-1, keepdims=True)
    ss_hi = jnp.sum(jnp.where(low, zero, sq), axis=-1, keepdims=True)
    inv = 1.0 / ATTN_HEAD_DIM
    return jnp.where(low, lax.rsqrt(ss_lo * inv + EPS), lax.rsqrt(ss_hi * inv + EPS))


def _attn_kernel(sink_ref, slope_ref, q_ref, kc_ref, kp_ref, vc_ref, vp_ref, qn_ref, kn_ref, o_ref):
    n = pl.program_id(1)
    blk = ATTN_BLOCK
    hd = ATTN_HEAD_DIM
    kf = jnp.concatenate([kp_ref[...], kc_ref[...]], axis=0).astype(F32)
    vf = jnp.concatenate([vp_ref[...], vc_ref[...]], axis=0).astype(F32)
    sj = lax.broadcasted_iota(jnp.int32, (2 * blk, blk), 0)
    qi = lax.broadcasted_iota(jnp.int32, (2 * blk, blk), 1)
    dist = qi + blk - sj
    valid = (dist >= 0) & (dist < WINDOW) & ((sj >= blk) | (n > 0))
    ndm = jnp.where(valid, -dist.astype(F32), NEG)
    npair = ATTN_Q_PER_KV // 2
    ndm4 = jnp.concatenate([ndm] * npair, axis=1)
    low_k = lax.broadcasted_iota(jnp.int32, (2 * blk, LANES), 1) < hd
    low_q = lax.broadcasted_iota(jnp.int32, (npair * blk, LANES), 1) < hd
    qn = qn_ref[...] * (LOG2E * ATTN_HEAD_DIM ** -0.5)
    kn = kn_ref[...]
    zk = jnp.zeros((2 * blk, LANES), F32)
    for kvp in range(ATTN_N_KV // 2):
        k2 = kf[:, kvp * LANES:(kvp + 1) * LANES]
        k2 = k2 * _half_rms_scale(k2, low_k) * kn
        v2 = vf[:, kvp * LANES:(kvp + 1) * LANES]
        k2r = pltpu.roll(k2, hd, axis=1)
        v2r = pltpu.roll(v2, hd, axis=1)
        for half in range(2):
            kv = 2 * kvp + half
            k_lo_src, k_hi_src = (k2, k2r) if half == 0 else (k2r, k2)
            v_lo_src, v_hi_src = (v2, v2r) if half == 0 else (v2r, v2)
            k_lo = jnp.where(low_k, k_lo_src, zk).astype(BF16)
            k_hi = jnp.where(low_k, zk, k_hi_src).astype(BF16)
            v_lo = jnp.where(low_k, v_lo_src, zk).astype(BF16)
            v_hi = jnp.where(low_k, zk, v_hi_src).astype(BF16)
            q4 = jnp.concatenate(
                [q_ref[:, (kv * npair + j) * LANES:(kv * npair + j + 1) * LANES] for j in range(npair)],
                axis=0).astype(F32)
            qb = (q4 * _half_rms_scale(q4, low_q) * qn).astype(BF16)
            o4 = None
            for e in range(2):
                u = 2 * kv + e
                st = _dot_nt(k_lo if e == 0 else k_hi, qb) + slope_ref[u:u + 1, :] * ndm4
                sink = sink_ref[u:u + 1, :] * LOG2E
                m = jnp.maximum(jnp.max(st, axis=0, keepdims=True), sink)
                p = jnp.exp2(st - m)
                denom = jnp.sum(p, axis=0, keepdims=True) + jnp.exp2(sink - m)
                pn = (p * (1.0 / denom)).astype(BF16)
                oe = lax.dot_general(pn, v_lo if e == 0 else v_hi, (((0,), (0,)), ((), ())),
                                     preferred_element_type=F32)
                o4 = oe if o4 is None else o4 + oe
            for j in range(npair):
                hp = kv * npair + j
                o_ref[:, hp * LANES:(hp + 1) * LANES] = o4[j * blk:(j + 1) * blk, :].astype(BF16)


def _head_table(per_head):
    npair = ATTN_Q_PER_KV // 2
    t = per_head.astype(F32).reshape(ATTN_N_KV, npair, 2).transpose(0, 2, 1)
    return jnp.repeat(t.reshape(2 * ATTN_N_KV, npair), LANES, axis=1)


def _attn(p, sinks, qn_row, kn_row, bsz, t_len):
    blk = ATTN_BLOCK
    slopes = jnp.asarray(
        [2.0 ** (-8.0 * (h + 1) / ATTN_N_HEADS) for h in range(ATTN_N_HEADS)], F32)
    nb = t_len // blk
    cur = lambda b, n: b * nb + n
    prev = lambda b, n: b * nb + jnp.maximum(n - 1, 0)
    return pl.pallas_call(
        _attn_kernel,
        grid=(bsz, nb),
        in_specs=[
            pl.BlockSpec((2 * ATTN_N_KV, 4 * LANES), lambda b, n: (0, 0)),
            pl.BlockSpec((2 * ATTN_N_KV, 4 * LANES), lambda b, n: (0, 0)),
            pl.BlockSpec((blk, ATTN_D), lambda b, n: (cur(b, n), P_Q // ATTN_D)),
            pl.BlockSpec((blk, ATTN_KV_D), lambda b, n: (cur(b, n), P_K // ATTN_KV_D)),
            pl.BlockSpec((blk, ATTN_KV_D), lambda b, n: (prev(b, n), P_K // ATTN_KV_D)),
            pl.BlockSpec((blk, ATTN_KV_D), lambda b, n: (cur(b, n), P_V // ATTN_KV_D)),
            pl.BlockSpec((blk, ATTN_KV_D), lambda b, n: (prev(b, n), P_V // ATTN_KV_D)),
            pl.BlockSpec((1, LANES), lambda b, n: (0, 0)),
            pl.BlockSpec((1, LANES), lambda b, n: (0, 0)),
        ],
        out_specs=pl.BlockSpec((blk, ATTN_D), lambda b, n: (cur(b, n), 0)),
        out_shape=jax.ShapeDtypeStruct((bsz * t_len, ATTN_D), BF16),
        compiler_params=_params(("parallel", "arbitrary")),
        name="swa",
    )(_head_table(sinks), _head_table(slopes * LOG2E), p, p, p, p, p, qn_row, kn_row)


def _merge_kernel(ys_ref, ya_ref, gs_ref, ga_ref, wos_ref, woa_ref, o_ref):
    ms = _dot(ys_ref[...], wos_ref[...])
    ma = _dot(ya_ref[...], woa_ref[...])
    mg = (jax.nn.sigmoid(gs_ref[...].astype(F32)) * ms
          + jax.nn.sigmoid(ga_ref[...].astype(F32)) * ma)
    o_ref[...] = mg.astype(BF16)


def _merge(ys, ya, p, wos, woa, tm, tn):
    m = ys.shape[0]
    d = wos.shape[1]
    return pl.pallas_call(
        _merge_kernel,
        grid=(m // tm, d // tn),
        in_specs=[
            pl.BlockSpec((tm, SSM_D_INNER), lambda i, j: (i, 0)),
            pl.BlockSpec((tm, ATTN_D), lambda i, j: (i, 0)),
            pl.BlockSpec((tm, tn), lambda i, j: (i, P_GS // tn + j)),
            pl.BlockSpec((tm, tn), lambda i, j: (i, P_GA // tn + j)),
            pl.BlockSpec((SSM_D_INNER, tn), lambda i, j: (0, j)),
            pl.BlockSpec((ATTN_D, tn), lambda i, j: (0, j)),
        ],
        out_specs=pl.BlockSpec((tm, tn), lambda i, j: (i, j)),
        out_shape=jax.ShapeDtypeStruct((m, d), BF16),
        compiler_params=_params(("parallel", "arbitrary")),
        name="merge",
    )(ys, ya, p, p, wos, woa)


def _outproj_kernel(x_ref, m_ref, w_ref, o_ref):
    o_ref[...] = x_ref[...] + _dot(m_ref[...], w_ref[...])


def _outproj(x, mg, wout, tm, tn):
    m, d = x.shape
    return pl.pallas_call(
        _outproj_kernel,
        grid=(m // tm, d // tn),
        in_specs=[
            pl.BlockSpec((tm, tn), lambda i, j: (i, j)),
            pl.BlockSpec((tm, d), lambda i, j: (i, 0)),
            pl.BlockSpec((d, tn), lambda i, j: (0, j)),
        ],
        out_specs=pl.BlockSpec((tm, tn), lambda i, j: (i, j)),
        out_shape=jax.ShapeDtypeStruct((m, d), F32),
        compiler_params=_params(("parallel", "arbitrary")),
        name="out_proj",
    )(x, mg, wout)


def _selection_constants():
    e = np.zeros((SSM_N_GROUPS, LANES, SSM_GROUP_WIDTH), np.float32)
    st = np.zeros((SSM_N_GROUPS, SSM_HEADS_PER_GROUP, LANES), np.float32)
    for g in range(SSM_N_GROUPS):
        for h in range(SSM_HEADS_PER_GROUP):
            e[g, g * SSM_HEADS_PER_GROUP + h, h * SSM_HEAD_DIM:(h + 1) * SSM_HEAD_DIM] = 1.0
            st[g, h, g * SSM_HEADS_PER_GROUP + h] = 1.0
    return jnp.asarray(e, BF16), jnp.asarray(st, BF16)


def _pad_lanes(v):
    return jnp.pad(v.astype(F32), (0, LANES - v.shape[0])).reshape(1, LANES)


def kernel(x, ffn1_norm, ffn1_w_gate, ffn1_w_up, ffn1_w_down, mix_norm, w_in, conv_w, conv_b, dt_bias, a_log, d_skip, ssm_norm, q_norm, k_norm, sinks, w_o_ssm, w_o_attn, w_out, ffn2_norm, ffn2_w_gate, ffn2_w_up, ffn2_w_down):
    bsz, t_len, d = x.shape
    m = bsz * t_len
    depth = ffn1_norm.shape[0]
    e01, selt = _selection_constants()
    xf = x.reshape(m, d)
    for l in range(depth):
        wi = w_in[l]
        w_main, w_dt = _pack_w_in(wi)

        xf = _ffn(xf, ffn1_norm[l].reshape(1, d), ffn1_w_gate[l].astype(BF16),
                  ffn1_w_up[l].astype(BF16), ffn1_w_down[l].astype(BF16), tm=1024, tf=512)

        p, dt_raw = _inproj(xf, mix_norm[l].reshape(1, d), w_main, w_dt, tm=1024, tn=1536)

        y_ssm = _ssd(
            p, dt_raw, conv_w[l], conv_b[l].reshape(1, -1),
            _pad_lanes(dt_bias[l]), _pad_lanes(a_log[l]),
            jnp.repeat(d_skip[l].astype(F32), SSM_HEAD_DIM).reshape(1, SSM_D_INNER),
            ssm_norm[l].reshape(1, SSM_D_INNER), e01, selt, bsz, t_len)

        y_attn = _attn(p, sinks[l].astype(F32),
                       jnp.tile(q_norm[l].astype(F32), LANES // ATTN_HEAD_DIM).reshape(1, LANES),
                       jnp.tile(k_norm[l].astype(F32), LANES // ATTN_HEAD_DIM).reshape(1, LANES),
                       bsz, t_len)

        mg = _merge(y_ssm, y_attn, p, w_o_ssm[l].astype(BF16), w_o_attn[l].astype(BF16),
                    tm=1024, tn=512)
        xf = _outproj(xf, mg, w_out[l].astype(BF16), tm=1024, tn=1024)

        xf = _ffn(xf, ffn2_norm[l].reshape(1, d), ffn2_w_gate[l].astype(BF16),
                  ffn2_w_up[l].astype(BF16), ffn2_w_down[l].astype(BF16), tm=1024, tf=512)
    return xf.reshape(bsz, t_len, d)
```

```python
import functools

import jax
import jax.numpy as jnp
import numpy as np
from jax import lax
from jax.experimental import pallas as pl
from jax.experimental.pallas import tpu as pltpu

F32 = jnp.float32
BF16 = jnp.bfloat16

D_MODEL = 2048
SSM_D_INNER = 4096
SSM_HEAD_DIM = 64
SSM_N_HEADS = 64
SSM_N_GROUPS = 8
SSM_HEADS_PER_GROUP = SSM_N_HEADS // SSM_N_GROUPS
SSM_GROUP_WIDTH = SSM_D_INNER // SSM_N_GROUPS
SSM_D_STATE = 128
SSM_CONV = 4
SSM_CHUNK = 256
SSM_GN = SSM_N_GROUPS * SSM_D_STATE
ATTN_HEAD_DIM = 64
ATTN_N_HEADS = 32
ATTN_N_KV = 4
ATTN_Q_PER_KV = 8
ATTN_D = ATTN_N_HEADS * ATTN_HEAD_DIM
ATTN_KV_D = ATTN_N_KV * ATTN_HEAD_DIM
WINDOW = 128
ATTN_BLOCK = 128
D_FF = 5632
EPS = 1e-6
NEG = -1e30
LOG2E = 1.4426950408889634

LANES = 128
CONV_HALO = 8

PA_Z = 0
PA_X = PA_Z + SSM_D_INNER
PA_B = PA_X + SSM_D_INNER
PA_C = PA_B + SSM_GN
PA_COLS = PA_C + SSM_GN
PB_Q = 0
PB_K = PB_Q + ATTN_D
PB_V = PB_K + ATTN_KV_D
PB_GS = PB_V + ATTN_KV_D
PB_GA = PB_GS + D_MODEL
PB_COLS = PB_GA + D_MODEL

W_DT = PA_COLS
W_Q = W_DT + SSM_N_HEADS

VMEM_LIMIT = 56 * 1024 * 1024


def _params(sem):
    return pltpu.CompilerParams(dimension_semantics=sem, vmem_limit_bytes=VMEM_LIMIT)


def _rms(x, gain):
    return x * lax.rsqrt(jnp.mean(x * x, axis=-1, keepdims=True) + EPS) * gain


def _silu(x):
    return x * jax.nn.sigmoid(x)


def _dot(a, b):
    return jnp.dot(a, b, preferred_element_type=F32)


def _dot_nt(a, b):
    return lax.dot_general(a, b, (((1,), (1,)), ((), ())), preferred_element_type=F32)


def _split3(x):
    hi = x.astype(BF16)
    r1 = x - hi.astype(F32)
    mid = r1.astype(BF16)
    lo = (r1 - mid.astype(F32)).astype(BF16)
    return hi, mid, lo


def _sel_left(m01, x):
    hi, mid, lo = _split3(x)
    return (_dot(m01, lo) + _dot(m01, mid)) + _dot(m01, hi)


def _sel_right(x, m01):
    hi, mid, lo = _split3(x)
    return (_dot(lo, m01) + _dot(mid, m01)) + _dot(hi, m01)


def _ffn_kernel(x_ref, gain_ref, wg_ref, wu_ref, wd_ref, o_ref, h_ref):
    j = pl.program_id(1)

    @pl.when(j == 0)
    def _():
        x = x_ref[...]
        h_ref[...] = _rms(x, gain_ref[...]).astype(BF16)
        o_ref[...] = x

    h = h_ref[...]
    g = _dot(h, wg_ref[...].astype(BF16))
    u = _dot(h, wu_ref[...].astype(BF16))
    a = (0.5 * _silu(g) * u).astype(BF16)
    o_ref[...] += _dot(a, wd_ref[...].astype(BF16))


def _ffn(x, gain, wg, wu, wd, tm, tf):
    m, d = x.shape
    dff = wg.shape[1]
    return pl.pallas_call(
        _ffn_kernel,
        grid=(m // tm, dff // tf),
        in_specs=[
            pl.BlockSpec((tm, d), lambda i, j: (i, 0), pipeline_mode=pl.Buffered(1)),
            pl.BlockSpec((1, d), lambda i, j: (0, 0)),
            pl.BlockSpec((d, tf), lambda i, j: (0, j)),
            pl.BlockSpec((d, tf), lambda i, j: (0, j)),
            pl.BlockSpec((tf, d), lambda i, j: (j, 0)),
        ],
        out_specs=pl.BlockSpec((tm, d), lambda i, j: (i, 0), pipeline_mode=pl.Buffered(1)),
        out_shape=jax.ShapeDtypeStruct((m, d), F32),
        scratch_shapes=[pltpu.VMEM((tm, d), BF16)],
        compiler_params=_params(("parallel", "arbitrary")),
        name="ffn",
    )(x, gain, wg, wu, wd)


def _inproj_ssm_kernel(x_ref, gain_ref, w_ref, wdt_ref, p_ref, dt_ref, h_ref):
    j = pl.program_id(1)

    @pl.when(j == 0)
    def _():
        h = _rms(x_ref[...], gain_ref[...]).astype(BF16)
        h_ref[...] = h
        dt = _dot_nt(h, wdt_ref[...].astype(BF16))
        lane = lax.broadcasted_iota(jnp.int32, dt.shape, 1)
        dt_ref[...] = jnp.where(lane < SSM_N_HEADS, dt, 0.0)

    p_ref[...] = _dot_nt(h_ref[...], w_ref[...].astype(BF16)).astype(BF16)


def _inproj_attn_kernel(x_ref, gain_ref, w_ref, p_ref, h_ref):
    j = pl.program_id(1)

    @pl.when(j == 0)
    def _():
        h_ref[...] = _rms(x_ref[...], gain_ref[...]).astype(BF16)

    p_ref[...] = _dot_nt(h_ref[...], w_ref[...].astype(BF16)).astype(BF16)


def _inproj_ssm(x, gain, wit, tm, tn):
    m, d = x.shape
    return pl.pallas_call(
        _inproj_ssm_kernel,
        grid=(m // tm, PA_COLS // tn),
        in_specs=[
            pl.BlockSpec((tm, d), lambda i, j: (i, 0)),
            pl.BlockSpec((1, d), lambda i, j: (0, 0)),
            pl.BlockSpec((tn, d), lambda i, j: (j, 0)),
            pl.BlockSpec((LANES, d), lambda i, j: (W_DT // LANES, 0)),
        ],
        out_specs=[
            pl.BlockSpec((tm, tn), lambda i, j: (i, j)),
            pl.BlockSpec((tm, LANES), lambda i, j: (i, 0)),
        ],
        out_shape=[
            jax.ShapeDtypeStruct((m, PA_COLS), BF16),
            jax.ShapeDtypeStruct((m, LANES), F32),
        ],
        scratch_shapes=[pltpu.VMEM((tm, d), BF16)],
        compiler_params=_params(("parallel", "arbitrary")),
        name="in_proj_ssm",
    )(x, gain, wit, wit)


def _inproj_attn(x, gain, wit, tm, tn):
    m, d = x.shape
    return pl.pallas_call(
        _inproj_attn_kernel,
        grid=(m // tm, PB_COLS // tn),
        in_specs=[
            pl.BlockSpec((tm, d), lambda i, j: (i, 0), pipeline_mode=pl.Buffered(1)),
            pl.BlockSpec((1, d), lambda i, j: (0, 0)),
            pl.BlockSpec((pl.Element(tn), pl.Element(d)), lambda i, j: (pl.multiple_of(W_Q + j * tn, SSM_N_HEADS), 0)),
        ],
        out_specs=pl.BlockSpec((tm, tn), lambda i, j: (i, j)),
        out_shape=jax.ShapeDtypeStruct((m, PB_COLS), BF16),
        scratch_shapes=[pltpu.VMEM((tm, d), BF16)],
        compiler_params=_params(("parallel", "arbitrary")),
        name="in_proj_attn",
    )(x, gain, wit)


def _ssd_kernel(z_ref, x_ref, b_ref, c_ref, dt_ref,
                cwx_ref, cwb_ref, cwc_ref, cbx_ref, cbb_ref, cbc_ref,
                dtb_ref, alog_ref, dsk_ref, nrm_ref, e_ref, selt_ref,
                o_ref, ext_ref, state_ref):
    c = pl.program_id(2)
    L = SSM_CHUNK
    GW = SSM_GROUP_WIDTH
    N = SSM_D_STATE
    XBC = GW + 2 * N

    @pl.when(c == 0)
    def _():
        ext_ref[0:CONV_HALO, :] = jnp.zeros((CONV_HALO, XBC), F32)
        state_ref[...] = jnp.zeros_like(state_ref)

    ext_ref[CONV_HALO:CONV_HALO + L, 0:GW] = x_ref[...].astype(F32)
    ext_ref[CONV_HALO:CONV_HALO + L, GW:GW + N] = b_ref[...].astype(F32)
    ext_ref[CONV_HALO:CONV_HALO + L, GW + N:XBC] = c_ref[...].astype(F32)

    cw = jnp.concatenate([cwx_ref[...], cwb_ref[...], cwc_ref[...]], axis=1)
    cbias = jnp.concatenate([cbx_ref[...], cbb_ref[...], cbc_ref[...]], axis=1)
    ext = ext_ref[...]
    acc = cw[0:1, :] * ext
    for k in range(1, SSM_CONV):
        acc = pltpu.roll(acc, 1, axis=0) + cw[k:k + 1, :] * ext
    ext_ref[0:CONV_HALO, :] = ext_ref[L:L + CONV_HALO, :]
    xbc = _silu(acc[CONV_HALO:, :] + cbias)
    xs = xbc[:, 0:GW]
    bm = xbc[:, GW:GW + N]
    cm = xbc[:, GW + N:XBC]

    dtr = dt_ref[...] + dtb_ref[...]
    dt = jnp.maximum(dtr, 0.0) + jnp.log1p(jnp.exp(-jnp.abs(dtr)))
    e01 = e_ref[...]
    dt_x = _sel_right(dt, e01)
    adt_c = dt * (-LOG2E * jnp.exp(alog_ref[...]))
    row = lax.broadcasted_iota(jnp.int32, (L, L), 0)
    col = lax.broadcasted_iota(jnp.int32, (L, L), 1)
    tril = row >= col
    tril01 = tril.astype(BF16)
    acum_c = _sel_left(tril01, adt_c)
    selt = selt_ref[...]
    hi, mid, lo = _split3(acum_c)
    acum_r = (_dot_nt(selt, lo) + _dot_nt(selt, mid)) + _dot_nt(selt, hi)
    acum_x = (_dot(lo, e01) + _dot(mid, e01)) + _dot(hi, e01)

    bt = bm.T.astype(BF16)
    cmb = cm.astype(BF16)
    H = L // 2
    cb_top = _dot(cmb[0:H, :], bt[:, 0:H])
    cb_bot = _dot(cmb[H:L, :], bt)
    xdt = xs * dt_x
    xdt_b = xdt.astype(BF16)

    tri = tril[0:H, 0:H]
    lane = lax.broadcasted_iota(jnp.int32, (L, LANES), 1)
    low_half = lane < SSM_HEAD_DIM
    pairs = []
    for pair in range(SSM_HEADS_PER_GROUP // 2):
        xp = xdt_b[:, pair * LANES:(pair + 1) * LANES]
        y_top = None
        y_bot = None
        for half in range(2):
            h = 2 * pair + half
            a_col = acum_x[:, h * SSM_HEAD_DIM:h * SSM_HEAD_DIM + 1]
            a_row = acum_r[h:h + 1, :]
            w00 = cb_top * jnp.exp2(jnp.where(tri, a_col[0:H] - a_row[:, 0:H], NEG))
            w10 = cb_bot[:, 0:H] * jnp.exp2(a_col[H:L] - a_row[:, 0:H])
            w11 = cb_bot[:, H:L] * jnp.exp2(jnp.where(tri, a_col[H:L] - a_row[:, H:L], NEG))
            keep = low_half if half == 0 else jnp.logical_not(low_half)
            xh = jnp.where(keep, xp, jnp.zeros_like(xp))
            yt = _dot(w00.astype(BF16), xh[0:H, :])
            yb = _dot(jnp.concatenate([w10, w11], axis=1).astype(BF16), xh)
            y_top = yt if y_top is None else y_top + yt
            y_bot = yb if y_bot is None else y_bot + yb
        pairs.append(jnp.concatenate([y_top, y_bot], axis=0))
    y_diag = jnp.concatenate(pairs, axis=1)

    state = state_ref[...]
    y_off = jnp.exp2(acum_x) * _dot(cmb, state.astype(BF16))
    y = y_diag + y_off + dsk_ref[...] * xs

    last = acum_x[L - 1:L, :]
    xdec = (xdt * jnp.exp2(last - acum_x)).astype(BF16)
    state_ref[...] = state * jnp.exp2(last) + _dot(bt, xdec)

    yg = y * _silu(z_ref[...].astype(F32))
    o_ref[...] = _rms(yg, nrm_ref[...]).astype(BF16)


def _ssd(p, dt_raw, conv_w, conv_b, dtb_row, alog_row, dskip_x, nrm_row, e01, selt, bsz, t_len):
    L = SSM_CHUNK
    nc = t_len // L
    GW = SSM_GROUP_WIDTH
    N = SSM_D_STATE
    rowblk = lambda b, g, c: b * nc + c
    return pl.pallas_call(
        _ssd_kernel,
        grid=(bsz, SSM_N_GROUPS, nc),
        in_specs=[
            pl.BlockSpec((L, GW), lambda b, g, c: (rowblk(b, g, c), PA_Z // GW + g)),
            pl.BlockSpec((L, GW), lambda b, g, c: (rowblk(b, g, c), PA_X // GW + g)),
            pl.BlockSpec((L, N), lambda b, g, c: (rowblk(b, g, c), PA_B // N + g)),
            pl.BlockSpec((L, N), lambda b, g, c: (rowblk(b, g, c), PA_C // N + g)),
            pl.BlockSpec((L, LANES), lambda b, g, c: (rowblk(b, g, c), 0)),
            pl.BlockSpec((SSM_CONV, GW), lambda b, g, c: (0, g)),
            pl.BlockSpec((SSM_CONV, N), lambda b, g, c: (0, SSM_D_INNER // N + g)),
            pl.BlockSpec((SSM_CONV, N), lambda b, g, c: (0, (SSM_D_INNER + SSM_GN) // N + g)),
            pl.BlockSpec((1, GW), lambda b, g, c: (0, g)),
            pl.BlockSpec((1, N), lambda b, g, c: (0, SSM_D_INNER // N + g)),
            pl.BlockSpec((1, N), lambda b, g, c: (0, (SSM_D_INNER + SSM_GN) // N + g)),
            pl.BlockSpec((1, LANES), lambda b, g, c: (0, 0)),
            pl.BlockSpec((1, LANES), lambda b, g, c: (0, 0)),
            pl.BlockSpec((1, GW), lambda b, g, c: (0, g)),
            pl.BlockSpec((1, GW), lambda b, g, c: (0, g)),
            pl.BlockSpec((None, LANES, GW), lambda b, g, c: (g, 0, 0)),
            pl.BlockSpec((None, SSM_HEADS_PER_GROUP, LANES), lambda b, g, c: (g, 0, 0)),
        ],
        out_specs=pl.BlockSpec((L, GW), lambda b, g, c: (rowblk(b, g, c), g)),
        out_shape=jax.ShapeDtypeStruct((bsz * t_len, SSM_D_INNER), BF16),
        scratch_shapes=[
            pltpu.VMEM((CONV_HALO + L, GW + 2 * N), F32),
            pltpu.VMEM((N, GW), F32),
        ],
        compiler_params=_params(("parallel", "parallel", "arbitrary")),
        name="ssd",
    )(p, p, p, p, dt_raw, conv_w, conv_w, conv_w, conv_b, conv_b, conv_b,
      dtb_row, alog_row, dskip_x, nrm_row, e01, selt)


def _half_rms_scale(x, low):
    sq = x * x
    zero = jnp.zeros_like(sq)
    ss_lo = jnp.sum(jnp.where(low, sq, zero), axis=-1, keepdims=True)
    ss_hi = jnp.sum(jnp.where(low, zero, sq), axis=-1, keepdims=True)
    inv = 1.0 / ATTN_HEAD_DIM
    return jnp.where(low, lax.rsqrt(ss_lo * inv + EPS), lax.rsqrt(ss_hi * inv + EPS))


def _attn_kernel(sink_ref, slope_ref, q_ref, kc_ref, kp_ref, vc_ref, vp_ref, qn_ref, kn_ref, o_ref):
    n = pl.program_id(1)
    blk = ATTN_BLOCK
    hd = ATTN_HEAD_DIM
    kf = jnp.concatenate([kp_ref[...], kc_ref[...]], axis=0).astype(F32)
    vf = jnp.concatenate([vp_ref[...], vc_ref[...]], axis=0).astype(F32)
    sj = lax.broadcasted_iota(jnp.int32, (2 * blk, blk), 0)
    qi = lax.broadcasted_iota(jnp.int32, (2 * blk, blk), 1)
    dist = qi + blk - sj
    valid = (dist >= 0) & (dist < WINDOW) & ((sj >= blk) | (n > 0))
    ndm = jnp.where(valid, -dist.astype(F32), NEG)
    npair = ATTN_Q_PER_KV // 2
    ndm4 = jnp.concatenate([ndm] * npair, axis=1)
    low_k = lax.broadcasted_iota(jnp.int32, (2 * blk, LANES), 1) < hd
    low_q = lax.broadcasted_iota(jnp.int32, (npair * blk, LANES), 1) < hd
    qn = qn_ref[...] * (LOG2E * ATTN_HEAD_DIM ** -0.5)
    kn = kn_ref[...]
    zk = jnp.zeros((2 * blk, LANES), F32)
    for kvp in range(ATTN_N_KV // 2):
        k2 = kf[:, kvp * LANES:(kvp + 1) * LANES]
        k2 = k2 * _half_rms_scale(k2, low_k) * kn
        v2 = vf[:, kvp * LANES:(kvp + 1) * LANES]
        k2r = pltpu.roll(k2, hd, axis=1)
        v2r = pltpu.roll(v2, hd, axis=1)
        for half in range(2):
            kv = 2 * kvp + half
            k_lo_src, k_hi_src = (k2, k2r) if half == 0 else (k2r, k2)
            v_lo_src, v_hi_src = (v2, v2r) if half == 0 else (v2r, v2)
            k_lo = jnp.where(low_k, k_lo_src, zk).astype(BF16)
            k_hi = jnp.where(low_k, zk, k_hi_src).astype(BF16)
            v_lo = jnp.where(low_k, v_lo_src, zk).astype(BF16)
            v_hi = jnp.where(low_k, zk, v_hi_src).astype(BF16)
            q4 = jnp.concatenate(
                [q_ref[:, (kv * npair + j) * LANES:(kv * npair + j + 1) * LANES] for j in range(npair)],
                axis=0).astype(F32)
            qb = (q4 * _half_rms_scale(q4, low_q) * qn).astype(BF16)
            o4 = None
            for e in range(2):
                u = 2 * kv + e
                st = _dot_nt(k_lo if e == 0 else k_hi, qb) + slope_ref[u:u + 1, :] * ndm4
                sink = sink_ref[u:u + 1, :] * LOG2E
                m = jnp.maximum(jnp.max(st, axis=0, keepdims=True), sink)
                p = jnp.exp2(st - m)
                denom = jnp.sum(p, axis=0, keepdims=True) + jnp.exp2(sink - m)
                pn = (p * (1.0 / denom)).astype(BF16)
                oe = lax.dot_general(pn, v_lo if e == 0 else v_hi, (((0,), (0,)), ((), ())),
                                     preferred_element_type=F32)
                o4 = oe if o4 is None else o4 + oe
            for j in range(npair):
                hp = kv * npair + j
                o_ref[:, hp * LANES:(hp + 1) * LANES] = o4[j * blk:(j + 1) * blk, :].astype(BF16)


def _head_table(per_head):
    npair = ATTN_Q_PER_KV // 2
    t = per_head.astype(F32).reshape(ATTN_N_KV, npair, 2).transpose(0, 2, 1)
    return jnp.repeat(t.reshape(2 * ATTN_N_KV, npair), LANES, axis=1)


def _attn(p, sinks, qn_row, kn_row, bsz, t_len):
    blk = ATTN_BLOCK
    slopes = jnp.asarray(
        [2.0 ** (-8.0 * (h + 1) / ATTN_N_HEADS) for h in range(ATTN_N_HEADS)], F32)
    nb = t_len // blk
    cur = lambda b, n: b * nb + n
    prev = lambda b, n: b * nb + jnp.maximum(n - 1, 0)
    return pl.pallas_call(
        _attn_kernel,
        grid=(bsz, nb),
        in_specs=[
            pl.BlockSpec((2 * ATTN_N_KV, 4 * LANES), lambda b, n: (0, 0)),
            pl.BlockSpec((2 * ATTN_N_KV, 4 * LANES), lambda b, n: (0, 0)),
            pl.BlockSpec((blk, ATTN_D), lambda b, n: (cur(b, n), PB_Q // ATTN_D)),
            pl.BlockSpec((blk, ATTN_KV_D), lambda b, n: (cur(b, n), PB_K // ATTN_KV_D)),
            pl.BlockSpec((blk, ATTN_KV_D), lambda b, n: (prev(b, n), PB_K // ATTN_KV_D)),
            pl.BlockSpec((blk, ATTN_KV_D), lambda b, n: (cur(b, n), PB_V // ATTN_KV_D)),
            pl.BlockSpec((blk, ATTN_KV_D), lambda b, n: (prev(b, n), PB_V // ATTN_KV_D)),
            pl.BlockSpec((1, LANES), lambda b, n: (0, 0)),
            pl.BlockSpec((1, LANES), lambda b, n: (0, 0)),
        ],
        out_specs=pl.BlockSpec((blk, ATTN_D), lambda b, n: (cur(b, n), 0)),
        out_shape=jax.ShapeDtypeStruct((bsz * t_len, ATTN_D), BF16),
        compiler_params=_params(("parallel", "arbitrary")),
        name="swa",
    )(_head_table(sinks), _head_table(slopes * LOG2E), p, p, p, p, p, qn_row, kn_row)


def _merge_kernel(ys_ref, ya_ref, gs_ref, ga_ref, wos_ref, woa_ref, o_ref):
    ms = _dot(ys_ref[...], wos_ref[...])
    ma = _dot(ya_ref[...], woa_ref[...])
    mg = (jax.nn.sigmoid(gs_ref[...].astype(F32)) * ms
          + jax.nn.sigmoid(ga_ref[...].astype(F32)) * ma)
    o_ref[...] = mg.astype(BF16)


def _merge(ys, ya, p, wos, woa, tm, tn):
    m = ys.shape[0]
    d = wos.shape[1]
    return pl.pallas_call(
        _merge_kernel,
        grid=(m // tm, d // tn),
        in_specs=[
            pl.BlockSpec((tm, SSM_D_INNER), lambda i, j: (i, 0)),
            pl.BlockSpec((tm, ATTN_D), lambda i, j: (i, 0)),
            pl.BlockSpec((tm, tn), lambda i, j: (i, PB_GS // tn + j)),
            pl.BlockSpec((tm, tn), lambda i, j: (i, PB_GA // tn + j)),
            pl.BlockSpec((SSM_D_INNER, tn), lambda i, j: (0, j)),
            pl.BlockSpec((ATTN_D, tn), lambda i, j: (0, j)),
        ],
        out_specs=pl.BlockSpec((tm, tn), lambda i, j: (i, j)),
        out_shape=jax.ShapeDtypeStruct((m, d), BF16),
        compiler_params=_params(("parallel", "arbitrary")),
        name="merge",
    )(ys, ya, p, p, wos, woa)


def _outproj_kernel(x_ref, m_ref, w_ref, o_ref):
    o_ref[...] = x_ref[...] + _dot(m_ref[...], w_ref[...].astype(BF16))


def _outproj(x, mg, wout, tm, tn):
    m, d = x.shape
    return pl.pallas_call(
        _outproj_kernel,
        grid=(m // tm, d // tn),
        in_specs=[
            pl.BlockSpec((tm, tn), lambda i, j: (i, j)),
            pl.BlockSpec((tm, d), lambda i, j: (i, 0)),
            pl.BlockSpec((d, tn), lambda i, j: (0, j)),
        ],
        out_specs=pl.BlockSpec((tm, tn), lambda i, j: (i, j)),
        out_shape=jax.ShapeDtypeStruct((m, d), F32),
        compiler_params=_params(("parallel", "arbitrary")),
        name="out_proj",
    )(x, mg, wout)


def _selection_constants():
    e = np.zeros((SSM_N_GROUPS, LANES, SSM_GROUP_WIDTH), np.float32)
    st = np.zeros((SSM_N_GROUPS, SSM_HEADS_PER_GROUP, LANES), np.float32)
    for g in range(SSM_N_GROUPS):
        for h in range(SSM_HEADS_PER_GROUP):
            e[g, g * SSM_HEADS_PER_GROUP + h, h * SSM_HEAD_DIM:(h + 1) * SSM_HEAD_DIM] = 1.0
            st[g, h, g * SSM_HEADS_PER_GROUP + h] = 1.0
    return jnp.asarray(e, BF16), jnp.asarray(st, BF16)


def _pad_lanes(v):
    return jnp.pad(v.astype(F32), (0, LANES - v.shape[0])).reshape(1, LANES)


def kernel(x, ffn1_norm, ffn1_w_gate, ffn1_w_up, ffn1_w_down, mix_norm, w_in, conv_w, conv_b, dt_bias, a_log, d_skip, ssm_norm, q_norm, k_norm, sinks, w_o_ssm, w_o_attn, w_out, ffn2_norm, ffn2_w_gate, ffn2_w_up, ffn2_w_down):
    bsz, t_len, d = x.shape
    m = bsz * t_len
    depth = ffn1_norm.shape[0]
    e01, selt = _selection_constants()
    xf = x.reshape(m, d)
    for l in range(depth):
        wit = jnp.swapaxes(w_in[l], 0, 1)

        xf = _ffn(xf, ffn1_norm[l].reshape(1, d), ffn1_w_gate[l], ffn1_w_up[l], ffn1_w_down[l],
                  tm=1024, tf=512)

        pa, dt_raw = _inproj_ssm(xf, mix_norm[l].reshape(1, d), wit, tm=1024, tn=1280)
        pb = _inproj_attn(xf, mix_norm[l].reshape(1, d), wit, tm=1024, tn=1664)

        y_ssm = _ssd(
            pa, dt_raw, conv_w[l], conv_b[l].reshape(1, -1),
            _pad_lanes(dt_bias[l]), _pad_lanes(a_log[l]),
            jnp.repeat(d_skip[l].astype(F32), SSM_HEAD_DIM).reshape(1, SSM_D_INNER),
            ssm_norm[l].reshape(1, SSM_D_INNER), e01, selt, bsz, t_len)

        y_attn = _attn(pb, sinks[l].astype(F32),
                       jnp.tile(q_norm[l].astype(F32), LANES // ATTN_HEAD_DIM).reshape(1, LANES),
                       jnp.tile(k_norm[l].astype(F32), LANES // ATTN_HEAD_DIM).reshape(1, LANES),
                       bsz, t_len)

        mg = _merge(y_ssm, y_attn, pb, w_o_ssm[l].astype(BF16), w_o_attn[l].astype(BF16),
                    tm=1024, tn=512)
        xf = _outproj(xf, mg, w_out[l], tm=1024, tn=1024)

        xf = _ffn(xf, ffn2_norm[l].reshape(1, d), ffn2_w_gate[l], ffn2_w_up[l], ffn2_w_down[l],
                  tm=1024, tf=512)
    return xf.reshape(bsz, t_len, d)
```

```python
import functools

import jax
import jax.numpy as jnp
import numpy as np
from jax import lax
from jax.experimental import pallas as pl
from jax.experimental.pallas import tpu as pltpu

F32 = jnp.float32
BF16 = jnp.bfloat16

D_MODEL = 2048
SSM_D_INNER = 4096
SSM_HEAD_DIM = 64
SSM_N_HEADS = 64
SSM_N_GROUPS = 8
SSM_HEADS_PER_GROUP = SSM_N_HEADS // SSM_N_GROUPS
SSM_GROUP_WIDTH = SSM_D_INNER // SSM_N_GROUPS
SSM_D_STATE = 128
SSM_CONV = 4
SSM_CHUNK = 256
SSM_GN = SSM_N_GROUPS * SSM_D_STATE
ATTN_HEAD_DIM = 64
ATTN_N_HEADS = 32
ATTN_N_KV = 4
ATTN_Q_PER_KV = 8
ATTN_D = ATTN_N_HEADS * ATTN_HEAD_DIM
ATTN_KV_D = ATTN_N_KV * ATTN_HEAD_DIM
WINDOW = 128
ATTN_BLOCK = 128
D_FF = 5632
EPS = 1e-6
NEG = -1e30
LOG2E = 1.4426950408889634

LANES = 128
CONV_HALO = 8

PA_Z = 0
PA_X = PA_Z + SSM_D_INNER
PA_B = PA_X + SSM_D_INNER
PA_C = PA_B + SSM_GN
PA_COLS = PA_C + SSM_GN
PB_Q = 0
PB_K = PB_Q + ATTN_D
PB_V = PB_K + ATTN_KV_D
PB_GS = PB_V + ATTN_KV_D
PB_GA = PB_GS + D_MODEL
PB_COLS = PB_GA + D_MODEL

W_DT = PA_COLS
W_Q = W_DT + SSM_N_HEADS

VMEM_LIMIT = 56 * 1024 * 1024


def _params(sem):
    return pltpu.CompilerParams(dimension_semantics=sem, vmem_limit_bytes=VMEM_LIMIT)


def _rms(x, gain):
    return x * lax.rsqrt(jnp.mean(x * x, axis=-1, keepdims=True) + EPS) * gain


def _silu(x):
    return x * jax.nn.sigmoid(x)


def _dot(a, b):
    return jnp.dot(a, b, preferred_element_type=F32)


def _dot_nt(a, b):
    return lax.dot_general(a, b, (((1,), (1,)), ((), ())), preferred_element_type=F32)


def _split3(x):
    hi = x.astype(BF16)
    r1 = x - hi.astype(F32)
    mid = r1.astype(BF16)
    lo = (r1 - mid.astype(F32)).astype(BF16)
    return hi, mid, lo


def _sel_left(m01, x):
    hi, mid, lo = _split3(x)
    return (_dot(m01, lo) + _dot(m01, mid)) + _dot(m01, hi)


def _sel_right(x, m01):
    hi, mid, lo = _split3(x)
    return (_dot(lo, m01) + _dot(mid, m01)) + _dot(hi, m01)


FFN_X_CHUNKS = 4


def _ffn_kernel(x_hbm, gain_ref, wg_ref, wu_ref, wd_ref, o_ref, h_ref, sems):
    i = pl.program_id(0)
    j = pl.program_id(1)
    tm = o_ref.shape[0]
    rows = tm // FFN_X_CHUNKS

    def x_copy(c):
        return pltpu.make_async_copy(
            x_hbm.at[pl.ds(i * tm + c * rows, rows), :], o_ref.at[pl.ds(c * rows, rows), :], sems.at[c])

    @pl.when(j == 0)
    def _():
        for c in range(FFN_X_CHUNKS):
            x_copy(c).start()
        for c in range(FFN_X_CHUNKS):
            x_copy(c).wait()
            xc = o_ref[pl.ds(c * rows, rows), :]
            h_ref[pl.ds(c * rows, rows), :] = _rms(xc, gain_ref[...]).astype(BF16)

    h = h_ref[...]
    g = _dot(h, wg_ref[...].astype(BF16))
    u = _dot(h, wu_ref[...].astype(BF16))
    a = (0.5 * _silu(g) * u).astype(BF16)
    o_ref[...] += _dot(a, wd_ref[...].astype(BF16))


def _ffn(x, gain, wg, wu, wd, tm, tf):
    m, d = x.shape
    dff = wg.shape[1]
    return pl.pallas_call(
        _ffn_kernel,
        grid=(m // tm, dff // tf),
        in_specs=[
            pl.BlockSpec(memory_space=pl.ANY),
            pl.BlockSpec((1, d), lambda i, j: (0, 0)),
            pl.BlockSpec((d, tf), lambda i, j: (0, j)),
            pl.BlockSpec((d, tf), lambda i, j: (0, j)),
            pl.BlockSpec((tf, d), lambda i, j: (j, 0)),
        ],
        out_specs=pl.BlockSpec((tm, d), lambda i, j: (i, 0)),
        out_shape=jax.ShapeDtypeStruct((m, d), F32),
        scratch_shapes=[pltpu.VMEM((tm, d), BF16), pltpu.SemaphoreType.DMA((FFN_X_CHUNKS,))],
        compiler_params=_params(("parallel", "arbitrary")),
        name="ffn",
    )(x, gain, wg, wu, wd)


def _inproj_ssm_kernel(x_ref, gain_ref, w_ref, wdt_ref, p_ref, dt_ref, h_ref):
    j = pl.program_id(1)

    @pl.when(j == 0)
    def _():
        h = _rms(x_ref[...], gain_ref[...]).astype(BF16)
        h_ref[...] = h
        dt = _dot_nt(h, wdt_ref[...].astype(BF16))
        lane = lax.broadcasted_iota(jnp.int32, dt.shape, 1)
        dt_ref[...] = jnp.where(lane < SSM_N_HEADS, dt, 0.0)

    p_ref[...] = _dot_nt(h_ref[...], w_ref[...].astype(BF16)).astype(BF16)


def _inproj_attn_kernel(x_hbm, gain_ref, w_ref, p_ref, h_ref, xbuf, sem):
    i = pl.program_id(0)
    j = pl.program_id(1)
    tm = xbuf.shape[0]

    def x_copy(tile):
        return pltpu.make_async_copy(x_hbm.at[pl.ds(tile * tm, tm), :], xbuf, sem)

    @pl.when((i == 0) & (j == 0))
    def _():
        x_copy(0).start()

    @pl.when(j == 0)
    def _():
        x_copy(i).wait()
        h_ref[...] = _rms(xbuf[...], gain_ref[...]).astype(BF16)

    @pl.when((j == 1) & (i + 1 < pl.num_programs(0)))
    def _():
        x_copy(i + 1).start()

    p_ref[...] = _dot_nt(h_ref[...], w_ref[...].astype(BF16)).astype(BF16)


def _inproj_ssm(x, gain, wit, tm, tn):
    m, d = x.shape
    return pl.pallas_call(
        _inproj_ssm_kernel,
        grid=(m // tm, PA_COLS // tn),
        in_specs=[
            pl.BlockSpec((tm, d), lambda i, j: (i, 0)),
            pl.BlockSpec((1, d), lambda i, j: (0, 0)),
            pl.BlockSpec((tn, d), lambda i, j: (j, 0)),
            pl.BlockSpec((LANES, d), lambda i, j: (W_DT // LANES, 0)),
        ],
        out_specs=[
            pl.BlockSpec((tm, tn), lambda i, j: (i, j)),
            pl.BlockSpec((tm, LANES), lambda i, j: (i, 0)),
        ],
        out_shape=[
            jax.ShapeDtypeStruct((m, PA_COLS), BF16),
            jax.ShapeDtypeStruct((m, LANES), F32),
        ],
        scratch_shapes=[pltpu.VMEM((tm, d), BF16)],
        compiler_params=_params(("parallel", "arbitrary")),
        name="in_proj_ssm",
    )(x, gain, wit, wit)


def _inproj_attn(x, gain, wit, tm, tn):
    m, d = x.shape
    return pl.pallas_call(
        _inproj_attn_kernel,
        grid=(m // tm, PB_COLS // tn),
        in_specs=[
            pl.BlockSpec(memory_space=pl.ANY),
            pl.BlockSpec((1, d), lambda i, j: (0, 0)),
            pl.BlockSpec((pl.Element(tn), pl.Element(d)), lambda i, j: (pl.multiple_of(W_Q + j * tn, SSM_N_HEADS), 0)),
        ],
        out_specs=pl.BlockSpec((tm, tn), lambda i, j: (i, j)),
        out_shape=jax.ShapeDtypeStruct((m, PB_COLS), BF16),
        scratch_shapes=[pltpu.VMEM((tm, d), BF16), pltpu.VMEM((tm, d), F32), pltpu.SemaphoreType.DMA(())],
        compiler_params=_params(("arbitrary", "arbitrary")),
        name="in_proj_attn",
    )(x, gain, wit)


def _ssd_kernel(z_ref, x_ref, b_ref, c_ref, dt_ref,
                cwx_ref, cwb_ref, cwc_ref, cbx_ref, cbb_ref, cbc_ref,
                dtb_ref, alog_ref, dsk_ref, nrm_ref, e_ref, selt_ref,
                o_ref, ext_ref, state_ref):
    c = pl.program_id(2)
    L = SSM_CHUNK
    GW = SSM_GROUP_WIDTH
    N = SSM_D_STATE
    XBC = GW + 2 * N

    @pl.when(c == 0)
    def _():
        ext_ref[0:CONV_HALO, :] = jnp.zeros((CONV_HALO, XBC), F32)
        state_ref[...] = jnp.zeros_like(state_ref)

    ext_ref[CONV_HALO:CONV_HALO + L, 0:GW] = x_ref[...].astype(F32)
    ext_ref[CONV_HALO:CONV_HALO + L, GW:GW + N] = b_ref[...].astype(F32)
    ext_ref[CONV_HALO:CONV_HALO + L, GW + N:XBC] = c_ref[...].astype(F32)

    cw = jnp.concatenate([cwx_ref[...], cwb_ref[...], cwc_ref[...]], axis=1)
    cbias = jnp.concatenate([cbx_ref[...], cbb_ref[...], cbc_ref[...]], axis=1)
    ext = ext_ref[...]
    acc = cw[0:1, :] * ext
    for k in range(1, SSM_CONV):
        acc = pltpu.roll(acc, 1, axis=0) + cw[k:k + 1, :] * ext
    ext_ref[0:CONV_HALO, :] = ext_ref[L:L + CONV_HALO, :]
    xbc = _silu(acc[CONV_HALO:, :] + cbias)
    xs = xbc[:, 0:GW]
    bm = xbc[:, GW:GW + N]
    cm = xbc[:, GW + N:XBC]

    dtr = dt_ref[...] + dtb_ref[...]
    dt = jnp.maximum(dtr, 0.0) + jnp.log1p(jnp.exp(-jnp.abs(dtr)))
    e01 = e_ref[...]
    dt_x = _sel_right(dt, e01)
    adt_c = dt * (-LOG2E * jnp.exp(alog_ref[...]))
    row = lax.broadcasted_iota(jnp.int32, (L, L), 0)
    col = lax.broadcasted_iota(jnp.int32, (L, L), 1)
    tril = row >= col
    tril01 = tril.astype(BF16)
    acum_c = _sel_left(tril01, adt_c)
    selt = selt_ref[...]
    hi, mid, lo = _split3(acum_c)
    acum_r = (_dot_nt(selt, lo) + _dot_nt(selt, mid)) + _dot_nt(selt, hi)
    acum_x = (_dot(lo, e01) + _dot(mid, e01)) + _dot(hi, e01)

    bt = bm.T.astype(BF16)
    cmb = cm.astype(BF16)
    H = L // 2
    cb_top = _dot(cmb[0:H, :], bt[:, 0:H])
    cb_bot = _dot(cmb[H:L, :], bt)
    xdt = xs * dt_x
    xdt_b = xdt.astype(BF16)

    tri = tril[0:H, 0:H]
    lane = lax.broadcasted_iota(jnp.int32, (L, LANES), 1)
    low_half = lane < SSM_HEAD_DIM
    pairs = []
    for pair in range(SSM_HEADS_PER_GROUP // 2):
        xp = xdt_b[:, pair * LANES:(pair + 1) * LANES]
        y_top = None
        y_bot = None
        for half in range(2):
            h = 2 * pair + half
            a_col = acum_x[:, h * SSM_HEAD_DIM:h * SSM_HEAD_DIM + 1]
            a_row = acum_r[h:h + 1, :]
            w00 = cb_top * jnp.exp2(jnp.where(tri, a_col[0:H] - a_row[:, 0:H], NEG))
            w10 = cb_bot[:, 0:H] * jnp.exp2(a_col[H:L] - a_row[:, 0:H])
            w11 = cb_bot[:, H:L] * jnp.exp2(jnp.where(tri, a_col[H:L] - a_row[:, H:L], NEG))
            keep = low_half if half == 0 else jnp.logical_not(low_half)
            xh = jnp.where(keep, xp, jnp.zeros_like(xp))
            yt = _dot(w00.astype(BF16), xh[0:H, :])
            yb = _dot(jnp.concatenate([w10, w11], axis=1).astype(BF16), xh)
            y_top = yt if y_top is None else y_top + yt
            y_bot = yb if y_bot is None else y_bot + yb
        pairs.append(jnp.concatenate([y_top, y_bot], axis=0))
    y_diag = jnp.concatenate(pairs, axis=1)

    state = state_ref[...]
    y_off = jnp.exp2(acum_x) * _dot(cmb, state.astype(BF16))
    y = y_diag + y_off + dsk_ref[...] * xs

    last = acum_x[L - 1:L, :]
    xdec = (xdt * jnp.exp2(last - acum_x)).astype(BF16)
    state_ref[...] = state * jnp.exp2(last) + _dot(bt, xdec)

    yg = y * _silu(z_ref[...].astype(F32))
    o_ref[...] = _rms(yg, nrm_ref[...]).astype(BF16)


def _ssd(p, dt_raw, conv_w, conv_b, dtb_row, alog_row, dskip_x, nrm_row, e01, selt, bsz, t_len):
    L = SSM_CHUNK
    nc = t_len // L
    GW = SSM_GROUP_WIDTH
    N = SSM_D_STATE
    rowblk = lambda b, g, c: b * nc + c
    return pl.pallas_call(
        _ssd_kernel,
        grid=(bsz, SSM_N_GROUPS, nc),
        in_specs=[
            pl.BlockSpec((L, GW), lambda b, g, c: (rowblk(b, g, c), PA_Z // GW + g)),
            pl.BlockSpec((L, GW), lambda b, g, c: (rowblk(b, g, c), PA_X // GW + g)),
            pl.BlockSpec((L, N), lambda b, g, c: (rowblk(b, g, c), PA_B // N + g)),
            pl.BlockSpec((L, N), lambda b, g, c: (rowblk(b, g, c), PA_C // N + g)),
            pl.BlockSpec((L, LANES), lambda b, g, c: (rowblk(b, g, c), 0)),
            pl.BlockSpec((SSM_CONV, GW), lambda b, g, c: (0, g)),
            pl.BlockSpec((SSM_CONV, N), lambda b, g, c: (0, SSM_D_INNER // N + g)),
            pl.BlockSpec((SSM_CONV, N), lambda b, g, c: (0, (SSM_D_INNER + SSM_GN) // N + g)),
            pl.BlockSpec((1, GW), lambda b, g, c: (0, g)),
            pl.BlockSpec((1, N), lambda b, g, c: (0, SSM_D_INNER // N + g)),
            pl.BlockSpec((1, N), lambda b, g, c: (0, (SSM_D_INNER + SSM_GN) // N + g)),
            pl.BlockSpec((1, LANES), lambda b, g, c: (0, 0)),
            pl.BlockSpec((1, LANES), lambda b, g, c: (0, 0)),
            pl.BlockSpec((1, GW), lambda b, g, c: (0, g)),
            pl.BlockSpec((1, GW), lambda b, g, c: (0, g)),
            pl.BlockSpec((None, LANES, GW), lambda b, g, c: (g, 0, 0)),
            pl.BlockSpec((None, SSM_HEADS_PER_GROUP, LANES), lambda b, g, c: (g, 0, 0)),
        ],
        out_specs=pl.BlockSpec((L, GW), lambda b, g, c: (rowblk(b, g, c), g)),
        out_shape=jax.ShapeDtypeStruct((bsz * t_len, SSM_D_INNER), BF16),
        scratch_shapes=[
            pltpu.VMEM((CONV_HALO + L, GW + 2 * N), F32),
            pltpu.VMEM((N, GW), F32),
        ],
        compiler_params=_params(("parallel", "parallel", "arbitrary")),
        name="ssd",
    )(p, p, p, p, dt_raw, conv_w, conv_w, conv_w, conv_b, conv_b, conv_b,
      dtb_row, alog_row, dskip_x, nrm_row, e01, selt)


def _half_rms_scale(x, low):
    sq = x * x
    zero = jnp.zeros_like(sq)
    ss_lo = jnp.sum(jnp.where(low, sq, zero), axis=-1, keepdims=True)
    ss_hi = jnp.sum(jnp.where(low, zero, sq), axis=-1, keepdims=True)
    inv = 1.0 / ATTN_HEAD_DIM
    return jnp.where(low, lax.rsqrt(ss_lo * inv + EPS), lax.rsqrt(ss_hi * inv + EPS))


def _attn_kernel(sink_ref, slope_ref, q_ref, kc_ref, kp_ref, vc_ref, vp_ref, qn_ref, kn_ref, o_ref):
    n = pl.program_id(1)
    blk = ATTN_BLOCK
    hd = ATTN_HEAD_DIM
    kf = jnp.concatenate([kp_ref[...], kc_ref[...]], axis=0).astype(F32)
    vf = jnp.concatenate([vp_ref[...], vc_ref[...]], axis=0).astype(F32)
    sj = lax.broadcasted_iota(jnp.int32, (2 * blk, blk), 0)
    qi = lax.broadcasted_iota(jnp.int32, (2 * blk, blk), 1)
    dist = qi + blk - sj
    valid = (dist >= 0) & (dist < WINDOW) & ((sj >= blk) | (n > 0))
    ndm = jnp.where(valid, -dist.astype(F32), NEG)
    npair = ATTN_Q_PER_KV // 2
    ndm4 = jnp.concatenate([ndm] * npair, axis=1)
    low_k = lax.broadcasted_iota(jnp.int32, (2 * blk, LANES), 1) < hd
    low_q = lax.broadcasted_iota(jnp.int32, (npair * blk, LANES), 1) < hd
    qn = qn_ref[...] * (LOG2E * ATTN_HEAD_DIM ** -0.5)
    kn = kn_ref[...]
    zk = jnp.zeros((2 * blk, LANES), F32)
    for kvp in range(ATTN_N_KV // 2):
        k2 = kf[:, kvp * LANES:(kvp + 1) * LANES]
        k2 = k2 * _half_rms_scale(k2, low_k) * kn
        v2 = vf[:, kvp * LANES:(kvp + 1) * LANES]
        k2r = pltpu.roll(k2, hd, axis=1)
        v2r = pltpu.roll(v2, hd, axis=1)
        for half in range(2):
            kv = 2 * kvp + half
            k_lo_src, k_hi_src = (k2, k2r) if half == 0 else (k2r, k2)
            v_lo_src, v_hi_src = (v2, v2r) if half == 0 else (v2r, v2)
            k_lo = jnp.where(low_k, k_lo_src, zk).astype(BF16)
            k_hi = jnp.where(low_k, zk, k_hi_src).astype(BF16)
            v_lo = jnp.where(low_k, v_lo_src, zk).astype(BF16)
            v_hi = jnp.where(low_k, zk, v_hi_src).astype(BF16)
            q4 = jnp.concatenate(
                [q_ref[:, (kv * npair + j) * LANES:(kv * npair + j + 1) * LANES] for j in range(npair)],
                axis=0).astype(F32)
            qb = (q4 * _half_rms_scale(q4, low_q) * qn).astype(BF16)
            o4 = None
            for e in range(2):
                u = 2 * kv + e
                st = _dot_nt(k_lo if e == 0 else k_hi, qb) + slope_ref[u:u + 1, :] * ndm4
                sink = sink_ref[u:u + 1, :] * LOG2E
                m = jnp.maximum(jnp.max(st, axis=0, keepdims=True), sink)
                p = jnp.exp2(st - m)
                denom = jnp.sum(p, axis=0, keepdims=True) + jnp.exp2(sink - m)
                pn = (p * (1.0 / denom)).astype(BF16)
                oe = lax.dot_general(pn, v_lo if e == 0 else v_hi, (((0,), (0,)), ((), ())),
                                     preferred_element_type=F32)
                o4 = oe if o4 is None else o4 + oe
            for j in range(npair):
                hp = kv * npair + j
                o_ref[:, hp * LANES:(hp + 1) * LANES] = o4[j * blk:(j + 1) * blk, :].astype(BF16)


def _head_table(per_head):
    npair = ATTN_Q_PER_KV // 2
    t = per_head.astype(F32).reshape(ATTN_N_KV, npair, 2).transpose(0, 2, 1)
    return jnp.repeat(t.reshape(2 * ATTN_N_KV, npair), LANES, axis=1)


def _attn(p, sinks, qn_row, kn_row, bsz, t_len):
    blk = ATTN_BLOCK
    slopes = jnp.asarray(
        [2.0 ** (-8.0 * (h + 1) / ATTN_N_HEADS) for h in range(ATTN_N_HEADS)], F32)
    nb = t_len // blk
    cur = lambda b, n: b * nb + n
    prev = lambda b, n: b * nb + jnp.maximum(n - 1, 0)
    return pl.pallas_call(
        _attn_kernel,
        grid=(bsz, nb),
        in_specs=[
            pl.BlockSpec((2 * ATTN_N_KV, 4 * LANES), lambda b, n: (0, 0)),
            pl.BlockSpec((2 * ATTN_N_KV, 4 * LANES), lambda b, n: (0, 0)),
            pl.BlockSpec((blk, ATTN_D), lambda b, n: (cur(b, n), PB_Q // ATTN_D)),
            pl.BlockSpec((blk, ATTN_KV_D), lambda b, n: (cur(b, n), PB_K // ATTN_KV_D)),
            pl.BlockSpec((blk, ATTN_KV_D), lambda b, n: (prev(b, n), PB_K // ATTN_KV_D)),
            pl.BlockSpec((blk, ATTN_KV_D), lambda b, n: (cur(b, n), PB_V // ATTN_KV_D)),
            pl.BlockSpec((blk, ATTN_KV_D), lambda b, n: (prev(b, n), PB_V // ATTN_KV_D)),
            pl.BlockSpec((1, LANES), lambda b, n: (0, 0)),
            pl.BlockSpec((1, LANES), lambda b, n: (0, 0)),
        ],
        out_specs=pl.BlockSpec((blk, ATTN_D), lambda b, n: (cur(b, n), 0)),
        out_shape=jax.ShapeDtypeStruct((bsz * t_len, ATTN_D), BF16),
        compiler_params=_params(("parallel", "arbitrary")),
        name="swa",
    )(_head_table(sinks), _head_table(slopes * LOG2E), p, p, p, p, p, qn_row, kn_row)


def _merge_kernel(ys_ref, ya_ref, gs_ref, ga_ref, wos_ref, woa_ref, o_ref):
    ms = _dot(ys_ref[...], wos_ref[...])
    ma = _dot(ya_ref[...], woa_ref[...])
    mg = (jax.nn.sigmoid(gs_ref[...].astype(F32)) * ms
          + jax.nn.sigmoid(ga_ref[...].astype(F32)) * ma)
    o_ref[...] = mg.astype(BF16)


def _merge(ys, ya, p, wos, woa, tm, tn):
    m = ys.shape[0]
    d = wos.shape[1]
    return pl.pallas_call(
        _merge_kernel,
        grid=(m // tm, d // tn),
        in_specs=[
            pl.BlockSpec((tm, SSM_D_INNER), lambda i, j: (i, 0)),
            pl.BlockSpec((tm, ATTN_D), lambda i, j: (i, 0)),
            pl.BlockSpec((tm, tn), lambda i, j: (i, PB_GS // tn + j)),
            pl.BlockSpec((tm, tn), lambda i, j: (i, PB_GA // tn + j)),
            pl.BlockSpec((SSM_D_INNER, tn), lambda i, j: (0, j)),
            pl.BlockSpec((ATTN_D, tn), lambda i, j: (0, j)),
        ],
        out_specs=pl.BlockSpec((tm, tn), lambda i, j: (i, j)),
        out_shape=jax.ShapeDtypeStruct((m, d), BF16),
        compiler_params=_params(("parallel", "arbitrary")),
        name="merge",
    )(ys, ya, p, p, wos, woa)


def _outproj_kernel(x_ref, m_ref, w_ref, o_ref, wb_ref):
    @pl.when(pl.program_id(0) == 0)
    def _():
        wb_ref[...] = w_ref[...].astype(BF16)

    o_ref[...] = x_ref[...] + _dot(m_ref[...], wb_ref[...])


def _outproj(x, mg, wout, tm):
    m, d = x.shape
    return pl.pallas_call(
        _outproj_kernel,
        grid=(m // tm,),
        in_specs=[
            pl.BlockSpec((tm, d), lambda i: (i, 0)),
            pl.BlockSpec((tm, d), lambda i: (i, 0)),
            pl.BlockSpec((d, d), lambda i: (0, 0), pipeline_mode=pl.Buffered(1)),
        ],
        out_specs=pl.BlockSpec((tm, d), lambda i: (i, 0)),
        out_shape=jax.ShapeDtypeStruct((m, d), F32),
        scratch_shapes=[pltpu.VMEM((d, d), BF16)],
        compiler_params=_params(("arbitrary",)),
        name="out_proj",
    )(x, mg, wout)


def _selection_constants():
    e = np.zeros((SSM_N_GROUPS, LANES, SSM_GROUP_WIDTH), np.float32)
    st = np.zeros((SSM_N_GROUPS, SSM_HEADS_PER_GROUP, LANES), np.float32)
    for g in range(SSM_N_GROUPS):
        for h in range(SSM_HEADS_PER_GROUP):
            e[g, g * SSM_HEADS_PER_GROUP + h, h * SSM_HEAD_DIM:(h + 1) * SSM_HEAD_DIM] = 1.0
            st[g, h, g * SSM_HEADS_PER_GROUP + h] = 1.0
    return jnp.asarray(e, BF16), jnp.asarray(st, BF16)


def _pad_lanes(v):
    return jnp.pad(v.astype(F32), (0, LANES - v.shape[0])).reshape(1, LANES)


def kernel(x, ffn1_norm, ffn1_w_gate, ffn1_w_up, ffn1_w_down, mix_norm, w_in, conv_w, conv_b, dt_bias, a_log, d_skip, ssm_norm, q_norm, k_norm, sinks, w_o_ssm, w_o_attn, w_out, ffn2_norm, ffn2_w_gate, ffn2_w_up, ffn2_w_down):
    bsz, t_len, d = x.shape
    m = bsz * t_len
    depth = ffn1_norm.shape[0]
    e01, selt = _selection_constants()
    xf = x.reshape(m, d)
    for l in range(depth):
        wit = jnp.swapaxes(w_in[l], 0, 1)

        xf = _ffn(xf, ffn1_norm[l].reshape(1, d), ffn1_w_gate[l], ffn1_w_up[l], ffn1_w_down[l],
                  tm=1024, tf=512)

        pa, dt_raw = _inproj_ssm(xf, mix_norm[l].reshape(1, d), wit, tm=1024, tn=1280)
        pb = _inproj_attn(xf, mix_norm[l].reshape(1, d), wit, tm=1024, tn=1664)

        y_ssm = _ssd(
            pa, dt_raw, conv_w[l], conv_b[l].reshape(1, -1),
            _pad_lanes(dt_bias[l]), _pad_lanes(a_log[l]),
            jnp.repeat(d_skip[l].astype(F32), SSM_HEAD_DIM).reshape(1, SSM_D_INNER),
            ssm_norm[l].reshape(1, SSM_D_INNER), e01, selt, bsz, t_len)

        y_attn = _attn(pb, sinks[l].astype(F32),
                       jnp.tile(q_norm[l].astype(F32), LANES // ATTN_HEAD_DIM).reshape(1, LANES),
                       jnp.tile(k_norm[l].astype(F32), LANES // ATTN_HEAD_DIM).reshape(1, LANES),
                       bsz, t_len)

        mg = _merge(y_ssm, y_attn, pb, w_o_ssm[l].astype(BF16), w_o_attn[l].astype(BF16),
                    tm=1024, tn=512)
        xf = _outproj(xf, mg, w_out[l], tm=512)

        xf = _ffn(xf, ffn2_norm[l].reshape(1, d), ffn2_w_gate[l], ffn2_w_up[l], ffn2_w_down[l],
                  tm=1024, tf=512)
    return xf.reshape(bsz, t_len, d)
```

```python
import functools

import jax
import jax.numpy as jnp
import numpy as np
from jax import lax
from jax.experimental import pallas as pl
from jax.experimental.pallas import tpu as pltpu

F32 = jnp.float32
BF16 = jnp.bfloat16

D_MODEL = 2048
SSM_D_INNER = 4096
SSM_HEAD_DIM = 64
SSM_N_HEADS = 64
SSM_N_GROUPS = 8
SSM_HEADS_PER_GROUP = SSM_N_HEADS // SSM_N_GROUPS
SSM_GROUP_WIDTH = SSM_D_INNER // SSM_N_GROUPS
SSM_D_STATE = 128
SSM_CONV = 4
SSM_CHUNK = 256
SSM_GN = SSM_N_GROUPS * SSM_D_STATE
ATTN_HEAD_DIM = 64
ATTN_N_HEADS = 32
ATTN_N_KV = 4
ATTN_Q_PER_KV = 8
ATTN_D = ATTN_N_HEADS * ATTN_HEAD_DIM
ATTN_KV_D = ATTN_N_KV * ATTN_HEAD_DIM
WINDOW = 128
ATTN_BLOCK = 128
D_FF = 5632
EPS = 1e-6
NEG = -1e30
LOG2E = 1.4426950408889634

LANES = 128
CONV_HALO = 8

PA_Z = 0
PA_X = PA_Z + SSM_D_INNER
PA_B = PA_X + SSM_D_INNER
PA_C = PA_B + SSM_GN
PA_COLS = PA_C + SSM_GN
PB_Q = 0
PB_K = PB_Q + ATTN_D
PB_V = PB_K + ATTN_KV_D
PB_GS = PB_V + ATTN_KV_D
PB_GA = PB_GS + D_MODEL
PB_COLS = PB_GA + D_MODEL

W_DT = PA_COLS
W_Q = W_DT + SSM_N_HEADS

VMEM_LIMIT = 56 * 1024 * 1024


def _params(sem):
    return pltpu.CompilerParams(dimension_semantics=sem, vmem_limit_bytes=VMEM_LIMIT)


def _rms(x, gain):
    return x * lax.rsqrt(jnp.mean(x * x, axis=-1, keepdims=True) + EPS) * gain


def _silu(x):
    return x * jax.nn.sigmoid(x)


def _dot(a, b):
    return jnp.dot(a, b, preferred_element_type=F32)


def _dot_nt(a, b):
    return lax.dot_general(a, b, (((1,), (1,)), ((), ())), preferred_element_type=F32)


def _split3(x):
    hi = x.astype(BF16)
    r1 = x - hi.astype(F32)
    mid = r1.astype(BF16)
    lo = (r1 - mid.astype(F32)).astype(BF16)
    return hi, mid, lo


def _sel_left(m01, x):
    hi, mid, lo = _split3(x)
    return (_dot(m01, lo) + _dot(m01, mid)) + _dot(m01, hi)


def _sel_right(x, m01):
    hi, mid, lo = _split3(x)
    return (_dot(lo, m01) + _dot(mid, m01)) + _dot(hi, m01)


FFN_X_CHUNKS = 4


def _ffn_kernel(x_hbm, gain_ref, wg_ref, wu_ref, wd_ref, o_ref, h_ref, sems):
    i = pl.program_id(0)
    j = pl.program_id(1)
    tm = o_ref.shape[0]
    rows = tm // FFN_X_CHUNKS

    def x_copy(c):
        return pltpu.make_async_copy(
            x_hbm.at[pl.ds(i * tm + c * rows, rows), :], o_ref.at[pl.ds(c * rows, rows), :], sems.at[c])

    @pl.when(j == 0)
    def _():
        for c in range(FFN_X_CHUNKS):
            x_copy(c).start()
        for c in range(FFN_X_CHUNKS):
            x_copy(c).wait()
            xc = o_ref[pl.ds(c * rows, rows), :]
            h_ref[pl.ds(c * rows, rows), :] = _rms(xc, gain_ref[...]).astype(BF16)

    def bf16_tile(w_ref):
        return w_ref[...] if w_ref.dtype == BF16 else w_ref[...].astype(BF16)

    h = h_ref[...]
    g = _dot(h, bf16_tile(wg_ref))
    u = _dot(h, bf16_tile(wu_ref))
    a = (0.5 * _silu(g) * u).astype(BF16)
    o_ref[...] += _dot(a, bf16_tile(wd_ref))


def _ffn(x, gain, wg, wu, wd, tm, tf):
    m, d = x.shape
    dff = wg.shape[1]
    return pl.pallas_call(
        _ffn_kernel,
        grid=(m // tm, dff // tf),
        in_specs=[
            pl.BlockSpec(memory_space=pl.ANY),
            pl.BlockSpec((1, d), lambda i, j: (0, 0)),
            pl.BlockSpec((d, tf), lambda i, j: (0, j)),
            pl.BlockSpec((d, tf), lambda i, j: (0, j)),
            pl.BlockSpec((tf, d), lambda i, j: (j, 0)),
        ],
        out_specs=pl.BlockSpec((tm, d), lambda i, j: (i, 0)),
        out_shape=jax.ShapeDtypeStruct((m, d), F32),
        scratch_shapes=[pltpu.VMEM((tm, d), BF16), pltpu.SemaphoreType.DMA((FFN_X_CHUNKS,))],
        compiler_params=_params(("parallel", "arbitrary")),
        name="ffn",
    )(x, gain, wg, wu, wd)


def _inproj_ssm_kernel(x_ref, gain_ref, w_ref, wdt_ref, p_ref, dt_ref, h_ref):
    j = pl.program_id(1)

    @pl.when(j == 0)
    def _():
        h = _rms(x_ref[...], gain_ref[...]).astype(BF16)
        h_ref[...] = h
        dt = _dot_nt(h, wdt_ref[...].astype(BF16))
        lane = lax.broadcasted_iota(jnp.int32, dt.shape, 1)
        dt_ref[...] = jnp.where(lane < SSM_N_HEADS, dt, 0.0)

    p_ref[...] = _dot_nt(h_ref[...], w_ref[...].astype(BF16)).astype(BF16)


def _inproj_attn_kernel(x_hbm, gain_ref, w_ref, p_ref, h_ref, xbuf, sem):
    i = pl.program_id(0)
    j = pl.program_id(1)
    tm = xbuf.shape[0]

    def x_copy(tile):
        return pltpu.make_async_copy(x_hbm.at[pl.ds(tile * tm, tm), :], xbuf, sem)

    @pl.when((i == 0) & (j == 0))
    def _():
        x_copy(0).start()

    @pl.when(j == 0)
    def _():
        x_copy(i).wait()
        h_ref[...] = _rms(xbuf[...], gain_ref[...]).astype(BF16)

    @pl.when((j == 1) & (i + 1 < pl.num_programs(0)))
    def _():
        x_copy(i + 1).start()

    p_ref[...] = _dot_nt(h_ref[...], w_ref[...].astype(BF16)).astype(BF16)


def _inproj_ssm(x, gain, wit, tm, tn):
    m, d = x.shape
    return pl.pallas_call(
        _inproj_ssm_kernel,
        grid=(m // tm, PA_COLS // tn),
        in_specs=[
            pl.BlockSpec((tm, d), lambda i, j: (i, 0)),
            pl.BlockSpec((1, d), lambda i, j: (0, 0)),
            pl.BlockSpec((tn, d), lambda i, j: (j, 0)),
            pl.BlockSpec((LANES, d), lambda i, j: (W_DT // LANES, 0)),
        ],
        out_specs=[
            pl.BlockSpec((tm, tn), lambda i, j: (i, j)),
            pl.BlockSpec((tm, LANES), lambda i, j: (i, 0)),
        ],
        out_shape=[
            jax.ShapeDtypeStruct((m, PA_COLS), BF16),
            jax.ShapeDtypeStruct((m, LANES), F32),
        ],
        scratch_shapes=[pltpu.VMEM((tm, d), BF16)],
        compiler_params=_params(("parallel", "arbitrary")),
        name="in_proj_ssm",
    )(x, gain, wit, wit)


def _inproj_attn(x, gain, wit, tm, tn):
    m, d = x.shape
    return pl.pallas_call(
        _inproj_attn_kernel,
        grid=(m // tm, PB_COLS // tn),
        in_specs=[
            pl.BlockSpec(memory_space=pl.ANY),
            pl.BlockSpec((1, d), lambda i, j: (0, 0)),
            pl.BlockSpec((pl.Element(tn), pl.Element(d)), lambda i, j: (pl.multiple_of(W_Q + j * tn, SSM_N_HEADS), 0)),
        ],
        out_specs=pl.BlockSpec((tm, tn), lambda i, j: (i, j)),
        out_shape=jax.ShapeDtypeStruct((m, PB_COLS), BF16),
        scratch_shapes=[pltpu.VMEM((tm, d), BF16), pltpu.VMEM((tm, d), F32), pltpu.SemaphoreType.DMA(())],
        compiler_params=_params(("arbitrary", "arbitrary")),
        name="in_proj_attn",
    )(x, gain, wit)


def _ssd_kernel(z_ref, x_ref, b_ref, c_ref, dt_ref,
                cwx_ref, cwb_ref, cwc_ref, cbx_ref, cbb_ref, cbc_ref,
                dtb_ref, alog_ref, dsk_ref, nrm_ref, e_ref, selt_ref, r0_ref, r1_ref, r2_ref,
                o_ref, q0_ref, q1_ref, q2_ref, ext_ref, state_ref):
    c = pl.program_id(2)
    L = SSM_CHUNK
    GW = SSM_GROUP_WIDTH
    N = SSM_D_STATE
    XBC = GW + 2 * N

    @pl.when(c == 0)
    def _():
        ext_ref[0:CONV_HALO, :] = jnp.zeros((CONV_HALO, XBC), F32)
        state_ref[...] = jnp.zeros_like(state_ref)

    ext_ref[CONV_HALO:CONV_HALO + L, 0:GW] = x_ref[...].astype(F32)
    ext_ref[CONV_HALO:CONV_HALO + L, GW:GW + N] = b_ref[...].astype(F32)
    ext_ref[CONV_HALO:CONV_HALO + L, GW + N:XBC] = c_ref[...].astype(F32)

    cw = jnp.concatenate([cwx_ref[...], cwb_ref[...], cwc_ref[...]], axis=1)
    cbias = jnp.concatenate([cbx_ref[...], cbb_ref[...], cbc_ref[...]], axis=1)
    ext = ext_ref[...]
    acc = cw[0:1, :] * ext
    for k in range(1, SSM_CONV):
        acc = pltpu.roll(acc, 1, axis=0) + cw[k:k + 1, :] * ext
    ext_ref[0:CONV_HALO, :] = ext_ref[L:L + CONV_HALO, :]
    xbc = _silu(acc[CONV_HALO:, :] + cbias)
    xs = xbc[:, 0:GW]
    bm = xbc[:, GW:GW + N]
    cm = xbc[:, GW + N:XBC]

    dtr = dt_ref[...] + dtb_ref[...]
    dt = jnp.maximum(dtr, 0.0) + jnp.log1p(jnp.exp(-jnp.abs(dtr)))
    e01 = e_ref[...]
    dt_x = _sel_right(dt, e01)
    adt_c = dt * (-LOG2E * jnp.exp(alog_ref[...]))
    row = lax.broadcasted_iota(jnp.int32, (L, L), 0)
    col = lax.broadcasted_iota(jnp.int32, (L, L), 1)
    tril = row >= col
    tril01 = tril.astype(BF16)
    acum_c = _sel_left(tril01, adt_c)
    selt = selt_ref[...]
    hi, mid, lo = _split3(acum_c)
    acum_r = (_dot_nt(selt, lo) + _dot_nt(selt, mid)) + _dot_nt(selt, hi)
    acum_x = (_dot(lo, e01) + _dot(mid, e01)) + _dot(hi, e01)

    bt = bm.T.astype(BF16)
    cmb = cm.astype(BF16)
    H = L // 2
    cb_top = _dot(cmb[0:H, :], bt[:, 0:H])
    cb_bot = _dot(cmb[H:L, :], bt)
    xdt = xs * dt_x
    xdt_b = xdt.astype(BF16)

    tri = tril[0:H, 0:H]
    lane = lax.broadcasted_iota(jnp.int32, (L, LANES), 1)
    low_half = lane < SSM_HEAD_DIM
    pairs = []
    for pair in range(SSM_HEADS_PER_GROUP // 2):
        xp = xdt_b[:, pair * LANES:(pair + 1) * LANES]
        y_top = None
        y_bot = None
        for half in range(2):
            h = 2 * pair + half
            a_col = acum_x[:, h * SSM_HEAD_DIM:h * SSM_HEAD_DIM + 1]
            a_row = acum_r[h:h + 1, :]
            w00 = cb_top * jnp.exp2(jnp.where(tri, a_col[0:H] - a_row[:, 0:H], NEG))
            w10 = cb_bot[:, 0:H] * jnp.exp2(a_col[H:L] - a_row[:, 0:H])
            w11 = cb_bot[:, H:L] * jnp.exp2(jnp.where(tri, a_col[H:L] - a_row[:, H:L], NEG))
            keep = low_half if half == 0 else jnp.logical_not(low_half)
            xh = jnp.where(keep, xp, jnp.zeros_like(xp))
            yt = _dot(w00.astype(BF16), xh[0:H, :])
            yb = _dot(jnp.concatenate([w10, w11], axis=1).astype(BF16), xh)
            y_top = yt if y_top is None else y_top + yt
            y_bot = yb if y_bot is None else y_bot + yb
        pairs.append(jnp.concatenate([y_top, y_bot], axis=0))
    y_diag = jnp.concatenate(pairs, axis=1)

    state = state_ref[...]
    y_off = jnp.exp2(acum_x) * _dot(cmb, state.astype(BF16))
    y = y_diag + y_off + dsk_ref[...] * xs

    last = acum_x[L - 1:L, :]
    xdec = (xdt * jnp.exp2(last - acum_x)).astype(BF16)
    state_ref[...] = state * jnp.exp2(last) + _dot(bt, xdec)

    yg = y * _silu(z_ref[...].astype(F32))
    o_ref[...] = _rms(yg, nrm_ref[...]).astype(BF16)

    @pl.when(c % 2 == 0)
    def _():
        q0_ref[...] = r0_ref[...].astype(BF16)
        q1_ref[...] = r1_ref[...].astype(BF16)
        q2_ref[...] = r2_ref[...].astype(BF16)


RIDER_ROWS = 16


def _ssd(p, dt_raw, conv_w, conv_b, dtb_row, alog_row, dskip_x, nrm_row, e01, selt, riders, bsz, t_len):
    L = SSM_CHUNK
    nc = t_len // L
    GW = SSM_GROUP_WIDTH
    N = SSM_D_STATE
    rowblk = lambda b, g, c: b * nc + c
    n_steps = bsz * SSM_N_GROUPS * nc
    visits = 2
    rider_specs = []
    for r in riders:
        assert r.shape[0] * visits == RIDER_ROWS * n_steps, r.shape
        rider_specs.append(pl.BlockSpec(
            (RIDER_ROWS, r.shape[1]), lambda b, g, c: (((b * SSM_N_GROUPS + g) * nc + c) // visits, 0)))
    outs = pl.pallas_call(
        _ssd_kernel,
        grid=(bsz, SSM_N_GROUPS, nc),
        in_specs=[
            pl.BlockSpec((L, GW), lambda b, g, c: (rowblk(b, g, c), PA_Z // GW + g)),
            pl.BlockSpec((L, GW), lambda b, g, c: (rowblk(b, g, c), PA_X // GW + g)),
            pl.BlockSpec((L, N), lambda b, g, c: (rowblk(b, g, c), PA_B // N + g)),
            pl.BlockSpec((L, N), lambda b, g, c: (rowblk(b, g, c), PA_C // N + g)),
            pl.BlockSpec((L, LANES), lambda b, g, c: (rowblk(b, g, c), 0)),
            pl.BlockSpec((SSM_CONV, GW), lambda b, g, c: (0, g)),
            pl.BlockSpec((SSM_CONV, N), lambda b, g, c: (0, SSM_D_INNER // N + g)),
            pl.BlockSpec((SSM_CONV, N), lambda b, g, c: (0, (SSM_D_INNER + SSM_GN) // N + g)),
            pl.BlockSpec((1, GW), lambda b, g, c: (0, g)),
            pl.BlockSpec((1, N), lambda b, g, c: (0, SSM_D_INNER // N + g)),
            pl.BlockSpec((1, N), lambda b, g, c: (0, (SSM_D_INNER + SSM_GN) // N + g)),
            pl.BlockSpec((1, LANES), lambda b, g, c: (0, 0)),
            pl.BlockSpec((1, LANES), lambda b, g, c: (0, 0)),
            pl.BlockSpec((1, GW), lambda b, g, c: (0, g)),
            pl.BlockSpec((1, GW), lambda b, g, c: (0, g)),
            pl.BlockSpec((None, LANES, GW), lambda b, g, c: (g, 0, 0)),
            pl.BlockSpec((None, SSM_HEADS_PER_GROUP, LANES), lambda b, g, c: (g, 0, 0)),
        ] + rider_specs,
        out_specs=[pl.BlockSpec((L, GW), lambda b, g, c: (rowblk(b, g, c), g))] + rider_specs,
        out_shape=[jax.ShapeDtypeStruct((bsz * t_len, SSM_D_INNER), BF16)]
        + [jax.ShapeDtypeStruct(r.shape, BF16) for r in riders],
        scratch_shapes=[
            pltpu.VMEM((CONV_HALO + L, GW + 2 * N), F32),
            pltpu.VMEM((N, GW), F32),
        ],
        compiler_params=_params(("parallel", "parallel", "arbitrary")),
        name="ssd",
    )(p, p, p, p, dt_raw, conv_w, conv_w, conv_w, conv_b, conv_b, conv_b,
      dtb_row, alog_row, dskip_x, nrm_row, e01, selt, *riders)
    return outs[0], outs[1:]


def _half_rms_scale(x, low):
    sq = x * x
    zero = jnp.zeros_like(sq)
    ss_lo = jnp.sum(jnp.where(low, sq, zero), axis=-1, keepdims=True)
    ss_hi = jnp.sum(jnp.where(low, zero, sq), axis=-1, keepdims=True)
    inv = 1.0 / ATTN_HEAD_DIM
    return jnp.where(low, lax.rsqrt(ss_lo * inv + EPS), lax.rsqrt(ss_hi * inv + EPS))


def _attn_kernel(sink_ref, slope_ref, q_ref, kc_ref, kp_ref, vc_ref, vp_ref, qn_ref, kn_ref, r0_ref, r1_ref,
                 o_ref, q0_ref, q1_ref):
    n = pl.program_id(1)
    blk = ATTN_BLOCK
    hd = ATTN_HEAD_DIM
    kf = jnp.concatenate([kp_ref[...], kc_ref[...]], axis=0).astype(F32)
    vf = jnp.concatenate([vp_ref[...], vc_ref[...]], axis=0).astype(F32)
    sj = lax.broadcasted_iota(jnp.int32, (2 * blk, blk), 0)
    qi = lax.broadcasted_iota(jnp.int32, (2 * blk, blk), 1)
    dist = qi + blk - sj
    valid = (dist >= 0) & (dist < WINDOW) & ((sj >= blk) | (n > 0))
    ndm = jnp.where(valid, -dist.astype(F32), NEG)
    npair = ATTN_Q_PER_KV // 2
    ndm4 = jnp.concatenate([ndm] * npair, axis=1)
    low_k = lax.broadcasted_iota(jnp.int32, (2 * blk, LANES), 1) < hd
    low_q = lax.broadcasted_iota(jnp.int32, (npair * blk, LANES), 1) < hd
    qn = qn_ref[...] * (LOG2E * ATTN_HEAD_DIM ** -0.5)
    kn = kn_ref[...]
    zk = jnp.zeros((2 * blk, LANES), F32)
    for kvp in range(ATTN_N_KV // 2):
        k2 = kf[:, kvp * LANES:(kvp + 1) * LANES]
        k2 = k2 * _half_rms_scale(k2, low_k) * kn
        v2 = vf[:, kvp * LANES:(kvp + 1) * LANES]
        k2r = pltpu.roll(k2, hd, axis=1)
        v2r = pltpu.roll(v2, hd, axis=1)
        for half in range(2):
            kv = 2 * kvp + half
            k_lo_src, k_hi_src = (k2, k2r) if half == 0 else (k2r, k2)
            v_lo_src, v_hi_src = (v2, v2r) if half == 0 else (v2r, v2)
            k_lo = jnp.where(low_k, k_lo_src, zk).astype(BF16)
            k_hi = jnp.where(low_k, zk, k_hi_src).astype(BF16)
            v_lo = jnp.where(low_k, v_lo_src, zk).astype(BF16)
            v_hi = jnp.where(low_k, zk, v_hi_src).astype(BF16)
            q4 = jnp.concatenate(
                [q_ref[:, (kv * npair + j) * LANES:(kv * npair + j + 1) * LANES] for j in range(npair)],
                axis=0).astype(F32)
            qb = (q4 * _half_rms_scale(q4, low_q) * qn).astype(BF16)
            o4 = None
            for e in range(2):
                u = 2 * kv + e
                st = _dot_nt(k_lo if e == 0 else k_hi, qb) + slope_ref[u:u + 1, :] * ndm4
                sink = sink_ref[u:u + 1, :] * LOG2E
                m = jnp.maximum(jnp.max(st, axis=0, keepdims=True), sink)
                p = jnp.exp2(st - m)
                denom = jnp.sum(p, axis=0, keepdims=True) + jnp.exp2(sink - m)
                pn = (p * (1.0 / denom)).astype(BF16)
                oe = lax.dot_general(pn, v_lo if e == 0 else v_hi, (((0,), (0,)), ((), ())),
                                     preferred_element_type=F32)
                o4 = oe if o4 is None else o4 + oe
            for j in range(npair):
                hp = kv * npair + j
                o_ref[:, hp * LANES:(hp + 1) * LANES] = o4[j * blk:(j + 1) * blk, :].astype(BF16)

    q0_ref[...] = r0_ref[...].astype(BF16)
    q1_ref[...] = r1_ref[...].astype(BF16)


def _head_table(per_head):
    npair = ATTN_Q_PER_KV // 2
    t = per_head.astype(F32).reshape(ATTN_N_KV, npair, 2).transpose(0, 2, 1)
    return jnp.repeat(t.reshape(2 * ATTN_N_KV, npair), LANES, axis=1)


def _attn(p, sinks, qn_row, kn_row, riders, bsz, t_len):
    blk = ATTN_BLOCK
    slopes = jnp.asarray(
        [2.0 ** (-8.0 * (h + 1) / ATTN_N_HEADS) for h in range(ATTN_N_HEADS)], F32)
    nb = t_len // blk
    cur = lambda b, n: b * nb + n
    prev = lambda b, n: b * nb + jnp.maximum(n - 1, 0)
    n_steps = bsz * nb
    rider_specs = []
    for r in riders:
        assert r.shape[0] % (n_steps * RIDER_ROWS) == 0, r.shape
        rider_specs.append(pl.BlockSpec((r.shape[0] // n_steps, r.shape[1]), lambda b, n: (cur(b, n), 0)))
    outs = pl.pallas_call(
        _attn_kernel,
        grid=(bsz, nb),
        in_specs=[
            pl.BlockSpec((2 * ATTN_N_KV, 4 * LANES), lambda b, n: (0, 0)),
            pl.BlockSpec((2 * ATTN_N_KV, 4 * LANES), lambda b, n: (0, 0)),
            pl.BlockSpec((blk, ATTN_D), lambda b, n: (cur(b, n), PB_Q // ATTN_D)),
            pl.BlockSpec((blk, ATTN_KV_D), lambda b, n: (cur(b, n), PB_K // ATTN_KV_D)),
            pl.BlockSpec((blk, ATTN_KV_D), lambda b, n: (prev(b, n), PB_K // ATTN_KV_D)),
            pl.BlockSpec((blk, ATTN_KV_D), lambda b, n: (cur(b, n), PB_V // ATTN_KV_D)),
            pl.BlockSpec((blk, ATTN_KV_D), lambda b, n: (prev(b, n), PB_V // ATTN_KV_D)),
            pl.BlockSpec((1, LANES), lambda b, n: (0, 0)),
            pl.BlockSpec((1, LANES), lambda b, n: (0, 0)),
        ] + rider_specs,
        out_specs=[pl.BlockSpec((blk, ATTN_D), lambda b, n: (cur(b, n), 0))] + rider_specs,
        out_shape=[jax.ShapeDtypeStruct((bsz * t_len, ATTN_D), BF16)]
        + [jax.ShapeDtypeStruct(r.shape, BF16) for r in riders],
        compiler_params=_params(("parallel", "arbitrary")),
        name="swa",
    )(_head_table(sinks), _head_table(slopes * LOG2E), p, p, p, p, p, qn_row, kn_row, *riders)
    return outs[0], outs[1:]


def _merge_kernel(ys_ref, ya_ref, gs_ref, ga_ref, wos_ref, woa_ref, o_ref):
    ms = _dot(ys_ref[...], wos_ref[...])
    ma = _dot(ya_ref[...], woa_ref[...])
    mg = (jax.nn.sigmoid(gs_ref[...].astype(F32)) * ms
          + jax.nn.sigmoid(ga_ref[...].astype(F32)) * ma)
    o_ref[...] = mg.astype(BF16)


def _merge(ys, ya, p, wos, woa, tm, tn):
    m = ys.shape[0]
    d = wos.shape[1]
    return pl.pallas_call(
        _merge_kernel,
        grid=(m // tm, d // tn),
        in_specs=[
            pl.BlockSpec((tm, SSM_D_INNER), lambda i, j: (i, 0)),
            pl.BlockSpec((tm, ATTN_D), lambda i, j: (i, 0)),
            pl.BlockSpec((tm, tn), lambda i, j: (i, PB_GS // tn + j)),
            pl.BlockSpec((tm, tn), lambda i, j: (i, PB_GA // tn + j)),
            pl.BlockSpec((SSM_D_INNER, tn), lambda i, j: (0, j)),
            pl.BlockSpec((ATTN_D, tn), lambda i, j: (0, j)),
        ],
        out_specs=pl.BlockSpec((tm, tn), lambda i, j: (i, j)),
        out_shape=jax.ShapeDtypeStruct((m, d), BF16),
        compiler_params=_params(("parallel", "arbitrary")),
        name="merge",
    )(ys, ya, p, p, wos, woa)


def _outproj_kernel(x_ref, m_ref, w_ref, o_ref, wb_ref):
    @pl.when(pl.program_id(0) == 0)
    def _():
        wb_ref[...] = w_ref[...].astype(BF16)

    o_ref[...] = x_ref[...] + _dot(m_ref[...], wb_ref[...])


def _outproj(x, mg, wout, tm):
    m, d = x.shape
    return pl.pallas_call(
        _outproj_kernel,
        grid=(m // tm,),
        in_specs=[
            pl.BlockSpec((tm, d), lambda i: (i, 0)),
            pl.BlockSpec((tm, d), lambda i: (i, 0)),
            pl.BlockSpec((d, d), lambda i: (0, 0), pipeline_mode=pl.Buffered(1)),
        ],
        out_specs=pl.BlockSpec((tm, d), lambda i: (i, 0)),
        out_shape=jax.ShapeDtypeStruct((m, d), F32),
        scratch_shapes=[pltpu.VMEM((d, d), BF16)],
        compiler_params=_params(("arbitrary",)),
        name="out_proj",
    )(x, mg, wout)


def _selection_constants():
    e = np.zeros((SSM_N_GROUPS, LANES, SSM_GROUP_WIDTH), np.float32)
    st = np.zeros((SSM_N_GROUPS, SSM_HEADS_PER_GROUP, LANES), np.float32)
    for g in range(SSM_N_GROUPS):
        for h in range(SSM_HEADS_PER_GROUP):
            e[g, g * SSM_HEADS_PER_GROUP + h, h * SSM_HEAD_DIM:(h + 1) * SSM_HEAD_DIM] = 1.0
            st[g, h, g * SSM_HEADS_PER_GROUP + h] = 1.0
    return jnp.asarray(e, BF16), jnp.asarray(st, BF16)


def _pad_lanes(v):
    return jnp.pad(v.astype(F32), (0, LANES - v.shape[0])).reshape(1, LANES)


def kernel(x, ffn1_norm, ffn1_w_gate, ffn1_w_up, ffn1_w_down, mix_norm, w_in, conv_w, conv_b, dt_bias, a_log, d_skip, ssm_norm, q_norm, k_norm, sinks, w_o_ssm, w_o_attn, w_out, ffn2_norm, ffn2_w_gate, ffn2_w_up, ffn2_w_down):
    bsz, t_len, d = x.shape
    m = bsz * t_len
    depth = ffn1_norm.shape[0]
    e01, selt = _selection_constants()
    xf = x.reshape(m, d)
    for l in range(depth):
        wit = jnp.swapaxes(w_in[l], 0, 1)

        xf = _ffn(xf, ffn1_norm[l].reshape(1, d), ffn1_w_gate[l], ffn1_w_up[l], ffn1_w_down[l],
                  tm=1024, tf=512)

        pa, dt_raw = _inproj_ssm(xf, mix_norm[l].reshape(1, d), wit, tm=1024, tn=1280)
        pb = _inproj_attn(xf, mix_norm[l].reshape(1, d), wit, tm=1024, tn=1664)

        dff = ffn2_w_gate.shape[2]
        y_ssm, (wg2, wu2, wd2) = _ssd(
            pa, dt_raw, conv_w[l], conv_b[l].reshape(1, -1),
            _pad_lanes(dt_bias[l]), _pad_lanes(a_log[l]),
            jnp.repeat(d_skip[l].astype(F32), SSM_HEAD_DIM).reshape(1, SSM_D_INNER),
            ssm_norm[l].reshape(1, SSM_D_INNER), e01, selt,
            (ffn2_w_gate[l], ffn2_w_up[l], ffn2_w_down[l].reshape(d, dff)), bsz, t_len)

        y_attn, (wos, woa) = _attn(
            pb, sinks[l].astype(F32),
            jnp.tile(q_norm[l].astype(F32), LANES // ATTN_HEAD_DIM).reshape(1, LANES),
            jnp.tile(k_norm[l].astype(F32), LANES // ATTN_HEAD_DIM).reshape(1, LANES),
            (w_o_ssm[l], w_o_attn[l]), bsz, t_len)

        mg = _merge(y_ssm, y_attn, pb, wos, woa, tm=1024, tn=512)
        xf = _outproj(xf, mg, w_out[l], tm=512)

        xf = _ffn(xf, ffn2_norm[l].reshape(1, d), wg2, wu2, wd2.reshape(dff, d), tm=1024, tf=512)
    return xf.reshape(bsz, t_len, d)
```

```python
import functools

import jax
import jax.numpy as jnp
import numpy as np
from jax import lax
from jax.experimental import pallas as pl
from jax.experimental.pallas import tpu as pltpu

F32 = jnp.float32
BF16 = jnp.bfloat16

D_MODEL = 2048
SSM_D_INNER = 4096
SSM_HEAD_DIM = 64
SSM_N_HEADS = 64
SSM_N_GROUPS = 8
SSM_HEADS_PER_GROUP = SSM_N_HEADS // SSM_N_GROUPS
SSM_GROUP_WIDTH = SSM_D_INNER // SSM_N_GROUPS
SSM_D_STATE = 128
SSM_CONV = 4
SSM_CHUNK = 256
SSM_GN = SSM_N_GROUPS * SSM_D_STATE
ATTN_HEAD_DIM = 64
ATTN_N_HEADS = 32
ATTN_N_KV = 4
ATTN_Q_PER_KV = 8
ATTN_D = ATTN_N_HEADS * ATTN_HEAD_DIM
ATTN_KV_D = ATTN_N_KV * ATTN_HEAD_DIM
WINDOW = 128
ATTN_BLOCK = 128
D_FF = 5632
EPS = 1e-6
NEG = -1e30
LOG2E = 1.4426950408889634

LANES = 128
CONV_HALO = 8

PA_Z = 0
PA_X = PA_Z + SSM_D_INNER
PA_B = PA_X + SSM_D_INNER
PA_C = PA_B + SSM_GN
PA_COLS = PA_C + SSM_GN
PB_Q = 0
PB_K = PB_Q + ATTN_D
PB_V = PB_K + ATTN_KV_D
PB_GS = PB_V + ATTN_KV_D
PB_GA = PB_GS + D_MODEL
PB_COLS = PB_GA + D_MODEL

W_DT = PA_COLS
W_Q = W_DT + SSM_N_HEADS

VMEM_LIMIT = 56 * 1024 * 1024


def _params(sem):
    return pltpu.CompilerParams(dimension_semantics=sem, vmem_limit_bytes=VMEM_LIMIT)


def _rms(x, gain):
    return x * lax.rsqrt(jnp.mean(x * x, axis=-1, keepdims=True) + EPS) * gain


def _silu(x):
    return x * jax.nn.sigmoid(x)


def _dot(a, b):
    return jnp.dot(a, b, preferred_element_type=F32)


def _dot_nt(a, b):
    return lax.dot_general(a, b, (((1,), (1,)), ((), ())), preferred_element_type=F32)


def _split3(x):
    hi = x.astype(BF16)
    r1 = x - hi.astype(F32)
    mid = r1.astype(BF16)
    lo = (r1 - mid.astype(F32)).astype(BF16)
    return hi, mid, lo


def _sel_left(m01, x):
    hi, mid, lo = _split3(x)
    return (_dot(m01, lo) + _dot(m01, mid)) + _dot(m01, hi)


def _sel_right(x, m01):
    hi, mid, lo = _split3(x)
    return (_dot(lo, m01) + _dot(mid, m01)) + _dot(hi, m01)


def _ffn_kernel(x_ref, gain_ref, wg_ref, wu_ref, wd_ref, o_ref, h_ref):
    j = pl.program_id(1)

    @pl.when(j == 0)
    def _():
        x = x_ref[...]
        h_ref[...] = _rms(x, gain_ref[...]).astype(BF16)
        o_ref[...] = x

    h = h_ref[...]
    g = _dot(h, wg_ref[...])
    u = _dot(h, wu_ref[...])
    a = (0.5 * _silu(g) * u).astype(BF16)
    o_ref[...] += _dot(a, wd_ref[...])


def _ffn(x, gain, wg, wu, wd, tm, tf):
    m, d = x.shape
    dff = wg.shape[1]
    return pl.pallas_call(
        _ffn_kernel,
        grid=(m // tm, dff // tf),
        in_specs=[
            pl.BlockSpec((tm, d), lambda i, j: (i, 0)),
            pl.BlockSpec((1, d), lambda i, j: (0, 0)),
            pl.BlockSpec((d, tf), lambda i, j: (0, j)),
            pl.BlockSpec((d, tf), lambda i, j: (0, j)),
            pl.BlockSpec((tf, d), lambda i, j: (j, 0)),
        ],
        out_specs=pl.BlockSpec((tm, d), lambda i, j: (i, 0)),
        out_shape=jax.ShapeDtypeStruct((m, d), F32),
        scratch_shapes=[pltpu.VMEM((tm, d), BF16)],
        compiler_params=_params(("parallel", "arbitrary")),
        name="ffn",
    )(x, gain, wg, wu, wd)


def _inproj_ssm_kernel(x_ref, gain_ref, w_ref, wdt_ref, p_ref, dt_ref, h_ref):
    j = pl.program_id(1)

    @pl.when(j == 0)
    def _():
        h = _rms(x_ref[...], gain_ref[...]).astype(BF16)
        h_ref[...] = h
        dt = _dot_nt(h, wdt_ref[...].astype(BF16))
        lane = lax.broadcasted_iota(jnp.int32, dt.shape, 1)
        dt_ref[...] = jnp.where(lane < SSM_N_HEADS, dt, 0.0)

    p_ref[...] = _dot_nt(h_ref[...], w_ref[...].astype(BF16)).astype(BF16)


def _inproj_attn_kernel(x_hbm, gain_ref, w_ref, p_ref, h_ref, xbuf, sem):
    i = pl.program_id(0)
    j = pl.program_id(1)
    tm = xbuf.shape[0]

    def x_copy(tile):
        return pltpu.make_async_copy(x_hbm.at[pl.ds(tile * tm, tm), :], xbuf, sem)

    @pl.when((i == 0) & (j == 0))
    def _():
        x_copy(0).start()

    @pl.when(j == 0)
    def _():
        x_copy(i).wait()
        h_ref[...] = _rms(xbuf[...], gain_ref[...]).astype(BF16)

    @pl.when((j == 1) & (i + 1 < pl.num_programs(0)))
    def _():
        x_copy(i + 1).start()

    p_ref[...] = _dot_nt(h_ref[...], w_ref[...].astype(BF16)).astype(BF16)


def _inproj_ssm(x, gain, wit, tm, tn):
    m, d = x.shape
    return pl.pallas_call(
        _inproj_ssm_kernel,
        grid=(m // tm, PA_COLS // tn),
        in_specs=[
            pl.BlockSpec((tm, d), lambda i, j: (i, 0)),
            pl.BlockSpec((1, d), lambda i, j: (0, 0)),
            pl.BlockSpec((tn, d), lambda i, j: (j, 0)),
            pl.BlockSpec((LANES, d), lambda i, j: (W_DT // LANES, 0)),
        ],
        out_specs=[
            pl.BlockSpec((tm, tn), lambda i, j: (i, j)),
            pl.BlockSpec((tm, LANES), lambda i, j: (i, 0)),
        ],
        out_shape=[
            jax.ShapeDtypeStruct((m, PA_COLS), BF16),
            jax.ShapeDtypeStruct((m, LANES), F32),
        ],
        scratch_shapes=[pltpu.VMEM((tm, d), BF16)],
        compiler_params=_params(("parallel", "arbitrary")),
        name="in_proj_ssm",
    )(x, gain, wit, wit)


def _inproj_attn(x, gain, wit, tm, tn):
    m, d = x.shape
    return pl.pallas_call(
        _inproj_attn_kernel,
        grid=(m // tm, PB_COLS // tn),
        in_specs=[
            pl.BlockSpec(memory_space=pl.ANY),
            pl.BlockSpec((1, d), lambda i, j: (0, 0)),
            pl.BlockSpec((pl.Element(tn), pl.Element(d)), lambda i, j: (pl.multiple_of(W_Q + j * tn, SSM_N_HEADS), 0)),
        ],
        out_specs=pl.BlockSpec((tm, tn), lambda i, j: (i, j)),
        out_shape=jax.ShapeDtypeStruct((m, PB_COLS), BF16),
        scratch_shapes=[pltpu.VMEM((tm, d), BF16), pltpu.VMEM((tm, d), F32), pltpu.SemaphoreType.DMA(())],
        compiler_params=_params(("arbitrary", "arbitrary")),
        name="in_proj_attn",
    )(x, gain, wit)


def _ssd_kernel(z_ref, x_ref, b_ref, c_ref, dt_ref,
                cwx_ref, cwb_ref, cwc_ref, cbx_ref, cbb_ref, cbc_ref,
                dtb_ref, alog_ref, dsk_ref, nrm_ref, e_ref, selt_ref, r0_ref, r1_ref, r2_ref,
                o_ref, q0_ref, q1_ref, q2_ref, ext_ref, state_ref, *, rider_spans):
    c = pl.program_id(2)
    L = SSM_CHUNK
    GW = SSM_GROUP_WIDTH
    N = SSM_D_STATE
    XBC = GW + 2 * N

    @pl.when(c == 0)
    def _():
        ext_ref[0:CONV_HALO, :] = jnp.zeros((CONV_HALO, XBC), F32)
        state_ref[...] = jnp.zeros_like(state_ref)

    ext_ref[CONV_HALO:CONV_HALO + L, 0:GW] = x_ref[...].astype(F32)
    ext_ref[CONV_HALO:CONV_HALO + L, GW:GW + N] = b_ref[...].astype(F32)
    ext_ref[CONV_HALO:CONV_HALO + L, GW + N:XBC] = c_ref[...].astype(F32)

    cw = jnp.concatenate([cwx_ref[...], cwb_ref[...], cwc_ref[...]], axis=1)
    cbias = jnp.concatenate([cbx_ref[...], cbb_ref[...], cbc_ref[...]], axis=1)
    ext = ext_ref[...]
    acc = cw[0:1, :] * ext
    for k in range(1, SSM_CONV):
        acc = pltpu.roll(acc, 1, axis=0) + cw[k:k + 1, :] * ext
    ext_ref[0:CONV_HALO, :] = ext_ref[L:L + CONV_HALO, :]
    xbc = _silu(acc[CONV_HALO:, :] + cbias)
    xs = xbc[:, 0:GW]
    bm = xbc[:, GW:GW + N]
    cm = xbc[:, GW + N:XBC]

    dtr = dt_ref[...] + dtb_ref[...]
    dt = jnp.maximum(dtr, 0.0) + jnp.log1p(jnp.exp(-jnp.abs(dtr)))
    e01 = e_ref[...]
    dt_x = _sel_right(dt, e01)
    adt_c = dt * (-LOG2E * jnp.exp(alog_ref[...]))
    row = lax.broadcasted_iota(jnp.int32, (L, L), 0)
    col = lax.broadcasted_iota(jnp.int32, (L, L), 1)
    tril = row >= col
    tril01 = tril.astype(BF16)
    acum_c = _sel_left(tril01, adt_c)
    selt = selt_ref[...]
    hi, mid, lo = _split3(acum_c)
    acum_r = (_dot_nt(selt, lo) + _dot_nt(selt, mid)) + _dot_nt(selt, hi)
    acum_x = (_dot(lo, e01) + _dot(mid, e01)) + _dot(hi, e01)

    bt = bm.T.astype(BF16)
    cmb = cm.astype(BF16)
    H = L // 2
    cb_top = _dot(cmb[0:H, :], bt[:, 0:H])
    cb_bot = _dot(cmb[H:L, :], bt)
    xdt = xs * dt_x
    xdt_b = xdt.astype(BF16)

    tri = tril[0:H, 0:H]
    lane = lax.broadcasted_iota(jnp.int32, (L, LANES), 1)
    low_half = lane < SSM_HEAD_DIM
    pairs = []
    for pair in range(SSM_HEADS_PER_GROUP // 2):
        xp = xdt_b[:, pair * LANES:(pair + 1) * LANES]
        y_top = None
        y_bot = None
        for half in range(2):
            h = 2 * pair + half
            a_col = acum_x[:, h * SSM_HEAD_DIM:h * SSM_HEAD_DIM + 1]
            a_row = acum_r[h:h + 1, :]
            w00 = cb_top * jnp.exp2(jnp.where(tri, a_col[0:H] - a_row[:, 0:H], NEG))
            w10 = cb_bot[:, 0:H] * jnp.exp2(a_col[H:L] - a_row[:, 0:H])
            w11 = cb_bot[:, H:L] * jnp.exp2(jnp.where(tri, a_col[H:L] - a_row[:, H:L], NEG))
            keep = low_half if half == 0 else jnp.logical_not(low_half)
            xh = jnp.where(keep, xp, jnp.zeros_like(xp))
            yt = _dot(w00.astype(BF16), xh[0:H, :])
            yb = _dot(jnp.concatenate([w10, w11], axis=1).astype(BF16), xh)
            y_top = yt if y_top is None else y_top + yt
            y_bot = yb if y_bot is None else y_bot + yb
        pairs.append(jnp.concatenate([y_top, y_bot], axis=0))
    y_diag = jnp.concatenate(pairs, axis=1)

    state = state_ref[...]
    y_off = jnp.exp2(acum_x) * _dot(cmb, state.astype(BF16))
    y = y_diag + y_off + dsk_ref[...] * xs

    last = acum_x[L - 1:L, :]
    xdec = (xdt * jnp.exp2(last - acum_x)).astype(BF16)
    state_ref[...] = state * jnp.exp2(last) + _dot(bt, xdec)

    yg = y * _silu(z_ref[...].astype(F32))
    o_ref[...] = _rms(yg, nrm_ref[...]).astype(BF16)

    for r_ref, q_ref, span in zip((r0_ref, r1_ref, r2_ref), (q0_ref, q1_ref, q2_ref), rider_spans):
        @pl.when(c % span == 0)
        def _(r_ref=r_ref, q_ref=q_ref):
            q_ref[...] = r_ref[...].astype(BF16)


RIDER_ROWS = 16


def _rider_span(rows, n_steps, max_span):
    span = 1
    while (rows * span) % (n_steps * RIDER_ROWS) != 0:
        span *= 2
        assert span <= max_span, (rows, n_steps)
    return span


def _ssd(p, dt_raw, conv_w, conv_b, dtb_row, alog_row, dskip_x, nrm_row, e01, selt, riders, bsz, t_len):
    L = SSM_CHUNK
    nc = t_len // L
    GW = SSM_GROUP_WIDTH
    N = SSM_D_STATE
    rowblk = lambda b, g, c: b * nc + c
    n_steps = bsz * SSM_N_GROUPS * nc
    spans = tuple(_rider_span(r.shape[0], n_steps, nc) for r in riders)
    rider_specs = [
        pl.BlockSpec((r.shape[0] * span // n_steps, r.shape[1]),
                     lambda b, g, c, span=span: (((b * SSM_N_GROUPS + g) * nc + c) // span, 0))
        for r, span in zip(riders, spans)]
    outs = pl.pallas_call(
        functools.partial(_ssd_kernel, rider_spans=spans),
        grid=(bsz, SSM_N_GROUPS, nc),
        in_specs=[
            pl.BlockSpec((L, GW), lambda b, g, c: (rowblk(b, g, c), PA_Z // GW + g)),
            pl.BlockSpec((L, GW), lambda b, g, c: (rowblk(b, g, c), PA_X // GW + g)),
            pl.BlockSpec((L, N), lambda b, g, c: (rowblk(b, g, c), PA_B // N + g)),
            pl.BlockSpec((L, N), lambda b, g, c: (rowblk(b, g, c), PA_C // N + g)),
            pl.BlockSpec((L, LANES), lambda b, g, c: (rowblk(b, g, c), 0)),
            pl.BlockSpec((SSM_CONV, GW), lambda b, g, c: (0, g)),
            pl.BlockSpec((SSM_CONV, N), lambda b, g, c: (0, SSM_D_INNER // N + g)),
            pl.BlockSpec((SSM_CONV, N), lambda b, g, c: (0, (SSM_D_INNER + SSM_GN) // N + g)),
            pl.BlockSpec((1, GW), lambda b, g, c: (0, g)),
            pl.BlockSpec((1, N), lambda b, g, c: (0, SSM_D_INNER // N + g)),
            pl.BlockSpec((1, N), lambda b, g, c: (0, (SSM_D_INNER + SSM_GN) // N + g)),
            pl.BlockSpec((1, LANES), lambda b, g, c: (0, 0)),
            pl.BlockSpec((1, LANES), lambda b, g, c: (0, 0)),
            pl.BlockSpec((1, GW), lambda b, g, c: (0, g)),
            pl.BlockSpec((1, GW), lambda b, g, c: (0, g)),
            pl.BlockSpec((None, LANES, GW), lambda b, g, c: (g, 0, 0)),
            pl.BlockSpec((None, SSM_HEADS_PER_GROUP, LANES), lambda b, g, c: (g, 0, 0)),
        ] + rider_specs,
        out_specs=[pl.BlockSpec((L, GW), lambda b, g, c: (rowblk(b, g, c), g))] + rider_specs,
        out_shape=[jax.ShapeDtypeStruct((bsz * t_len, SSM_D_INNER), BF16)]
        + [jax.ShapeDtypeStruct(r.shape, BF16) for r in riders],
        scratch_shapes=[
            pltpu.VMEM((CONV_HALO + L, GW + 2 * N), F32),
            pltpu.VMEM((N, GW), F32),
        ],
        compiler_params=_params(("parallel", "parallel", "arbitrary")),
        name="ssd",
    )(p, p, p, p, dt_raw, conv_w, conv_w, conv_w, conv_b, conv_b, conv_b,
      dtb_row, alog_row, dskip_x, nrm_row, e01, selt, *riders)
    return outs[0], outs[1:]


def _half_rms_scale(x, low):
    sq = x * x
    zero = jnp.zeros_like(sq)
    ss_lo = jnp.sum(jnp.where(low, sq, zero), axis=-1, keepdims=True)
    ss_hi = jnp.sum(jnp.where(low, zero, sq), axis=-1, keepdims=True)
    inv = 1.0 / ATTN_HEAD_DIM
    return jnp.where(low, lax.rsqrt(ss_lo * inv + EPS), lax.rsqrt(ss_hi * inv + EPS))


def _attn_kernel(sink_ref, slope_ref, q_ref, kc_ref, kp_ref, vc_ref, vp_ref, qn_ref, kn_ref, r0_ref, r1_ref,
                 o_ref, q0_ref, q1_ref):
    n = pl.program_id(1)
    blk = ATTN_BLOCK
    hd = ATTN_HEAD_DIM
    kf = jnp.concatenate([kp_ref[...], kc_ref[...]], axis=0).astype(F32)
    vf = jnp.concatenate([vp_ref[...], vc_ref[...]], axis=0).astype(F32)
    sj = lax.broadcasted_iota(jnp.int32, (2 * blk, blk), 0)
    qi = lax.broadcasted_iota(jnp.int32, (2 * blk, blk), 1)
    dist = qi + blk - sj
    valid = (dist >= 0) & (dist < WINDOW) & ((sj >= blk) | (n > 0))
    ndm = jnp.where(valid, -dist.astype(F32), NEG)
    npair = ATTN_Q_PER_KV // 2
    ndm4 = jnp.concatenate([ndm] * npair, axis=1)
    low_k = lax.broadcasted_iota(jnp.int32, (2 * blk, LANES), 1) < hd
    low_q = lax.broadcasted_iota(jnp.int32, (npair * blk, LANES), 1) < hd
    qn = qn_ref[...] * (LOG2E * ATTN_HEAD_DIM ** -0.5)
    kn = kn_ref[...]
    zk = jnp.zeros((2 * blk, LANES), F32)
    for kvp in range(ATTN_N_KV // 2):
        k2 = kf[:, kvp * LANES:(kvp + 1) * LANES]
        k2 = k2 * _half_rms_scale(k2, low_k) * kn
        v2 = vf[:, kvp * LANES:(kvp + 1) * LANES]
        k2r = pltpu.roll(k2, hd, axis=1)
        v2r = pltpu.roll(v2, hd, axis=1)
        for half in range(2):
            kv = 2 * kvp + half
            k_lo_src, k_hi_src = (k2, k2r) if half == 0 else (k2r, k2)
            v_lo_src, v_hi_src = (v2, v2r) if half == 0 else (v2r, v2)
            k_lo = jnp.where(low_k, k_lo_src, zk).astype(BF16)
            k_hi = jnp.where(low_k, zk, k_hi_src).astype(BF16)
            v_lo = jnp.where(low_k, v_lo_src, zk).astype(BF16)
            v_hi = jnp.where(low_k, zk, v_hi_src).astype(BF16)
            q4 = jnp.concatenate(
                [q_ref[:, (kv * npair + j) * LANES:(kv * npair + j + 1) * LANES] for j in range(npair)],
                axis=0).astype(F32)
            qb = (q4 * _half_rms_scale(q4, low_q) * qn).astype(BF16)
            o4 = None
            for e in range(2):
                u = 2 * kv + e
                st = _dot_nt(k_lo if e == 0 else k_hi, qb) + slope_ref[u:u + 1, :] * ndm4
                sink = sink_ref[u:u + 1, :] * LOG2E
                m = jnp.maximum(jnp.max(st, axis=0, keepdims=True), sink)
                p = jnp.exp2(st - m)
                denom = jnp.sum(p, axis=0, keepdims=True) + jnp.exp2(sink - m)
                pn = (p * (1.0 / denom)).astype(BF16)
                oe = lax.dot_general(pn, v_lo if e == 0 else v_hi, (((0,), (0,)), ((), ())),
                                     preferred_element_type=F32)
                o4 = oe if o4 is None else o4 + oe
            for j in range(npair):
                hp = kv * npair + j
                o_ref[:, hp * LANES:(hp + 1) * LANES] = o4[j * blk:(j + 1) * blk, :].astype(BF16)

    q0_ref[...] = r0_ref[...].astype(BF16)
    q1_ref[...] = r1_ref[...].astype(BF16)


def _head_table(per_head):
    npair = ATTN_Q_PER_KV // 2
    t = per_head.astype(F32).reshape(ATTN_N_KV, npair, 2).transpose(0, 2, 1)
    return jnp.repeat(t.reshape(2 * ATTN_N_KV, npair), LANES, axis=1)


def _attn(p, sinks, qn_row, kn_row, riders, bsz, t_len):
    blk = ATTN_BLOCK
    slopes = jnp.asarray(
        [2.0 ** (-8.0 * (h + 1) / ATTN_N_HEADS) for h in range(ATTN_N_HEADS)], F32)
    nb = t_len // blk
    cur = lambda b, n: b * nb + n
    prev = lambda b, n: b * nb + jnp.maximum(n - 1, 0)
    n_steps = bsz * nb
    rider_specs = []
    for r in riders:
        assert r.shape[0] % (n_steps * RIDER_ROWS) == 0, r.shape
        rider_specs.append(pl.BlockSpec((r.shape[0] // n_steps, r.shape[1]), lambda b, n: (cur(b, n), 0)))
    outs = pl.pallas_call(
        _attn_kernel,
        grid=(bsz, nb),
        in_specs=[
            pl.BlockSpec((2 * ATTN_N_KV, 4 * LANES), lambda b, n: (0, 0)),
            pl.BlockSpec((2 * ATTN_N_KV, 4 * LANES), lambda b, n: (0, 0)),
            pl.BlockSpec((blk, ATTN_D), lambda b, n: (cur(b, n), PB_Q // ATTN_D)),
            pl.BlockSpec((blk, ATTN_KV_D), lambda b, n: (cur(b, n), PB_K // ATTN_KV_D)),
            pl.BlockSpec((blk, ATTN_KV_D), lambda b, n: (prev(b, n), PB_K // ATTN_KV_D)),
            pl.BlockSpec((blk, ATTN_KV_D), lambda b, n: (cur(b, n), PB_V // ATTN_KV_D)),
            pl.BlockSpec((blk, ATTN_KV_D), lambda b, n: (prev(b, n), PB_V // ATTN_KV_D)),
            pl.BlockSpec((1, LANES), lambda b, n: (0, 0)),
            pl.BlockSpec((1, LANES), lambda b, n: (0, 0)),
        ] + rider_specs,
        out_specs=[pl.BlockSpec((blk, ATTN_D), lambda b, n: (cur(b, n), 0))] + rider_specs,
        out_shape=[jax.ShapeDtypeStruct((bsz * t_len, ATTN_D), BF16)]
        + [jax.ShapeDtypeStruct(r.shape, BF16) for r in riders],
        compiler_params=_params(("parallel", "arbitrary")),
        name="swa",
    )(_head_table(sinks), _head_table(slopes * LOG2E), p, p, p, p, p, qn_row, kn_row, *riders)
    return outs[0], outs[1:]


def _merge_kernel(ys_ref, ya_ref, gs_ref, ga_ref, wos_ref, woa_ref, o_ref):
    ms = _dot(ys_ref[...], wos_ref[...])
    ma = _dot(ya_ref[...], woa_ref[...])
    mg = (jax.nn.sigmoid(gs_ref[...].astype(F32)) * ms
          + jax.nn.sigmoid(ga_ref[...].astype(F32)) * ma)
    o_ref[...] = mg.astype(BF16)


def _merge(ys, ya, p, wos, woa, tm, tn):
    m = ys.shape[0]
    d = wos.shape[1]
    return pl.pallas_call(
        _merge_kernel,
        grid=(m // tm, d // tn),
        in_specs=[
            pl.BlockSpec((tm, SSM_D_INNER), lambda i, j: (i, 0)),
            pl.BlockSpec((tm, ATTN_D), lambda i, j: (i, 0)),
            pl.BlockSpec((tm, tn), lambda i, j: (i, PB_GS // tn + j)),
            pl.BlockSpec((tm, tn), lambda i, j: (i, PB_GA // tn + j)),
            pl.BlockSpec((SSM_D_INNER, tn), lambda i, j: (0, j)),
            pl.BlockSpec((ATTN_D, tn), lambda i, j: (0, j)),
        ],
        out_specs=pl.BlockSpec((tm, tn), lambda i, j: (i, j)),
        out_shape=jax.ShapeDtypeStruct((m, d), BF16),
        compiler_params=_params(("parallel", "arbitrary")),
        name="merge",
    )(ys, ya, p, p, wos, woa)


def _outproj_kernel(x_ref, m_ref, w_ref, o_ref, wb_ref):
    @pl.when(pl.program_id(0) == 0)
    def _():
        wb_ref[...] = w_ref[...].astype(BF16)

    o_ref[...] = x_ref[...] + _dot(m_ref[...], wb_ref[...])


def _outproj(x, mg, wout, tm):
    m, d = x.shape
    return pl.pallas_call(
        _outproj_kernel,
        grid=(m // tm,),
        in_specs=[
            pl.BlockSpec((tm, d), lambda i: (i, 0)),
            pl.BlockSpec((tm, d), lambda i: (i, 0)),
            pl.BlockSpec((d, d), lambda i: (0, 0), pipeline_mode=pl.Buffered(1)),
        ],
        out_specs=pl.BlockSpec((tm, d), lambda i: (i, 0)),
        out_shape=jax.ShapeDtypeStruct((m, d), F32),
        scratch_shapes=[pltpu.VMEM((d, d), BF16)],
        compiler_params=_params(("arbitrary",)),
        name="out_proj",
    )(x, mg, wout)


def _selection_constants():
    e = np.zeros((SSM_N_GROUPS, LANES, SSM_GROUP_WIDTH), np.float32)
    st = np.zeros((SSM_N_GROUPS, SSM_HEADS_PER_GROUP, LANES), np.float32)
    for g in range(SSM_N_GROUPS):
        for h in range(SSM_HEADS_PER_GROUP):
            e[g, g * SSM_HEADS_PER_GROUP + h, h * SSM_HEAD_DIM:(h + 1) * SSM_HEAD_DIM] = 1.0
            st[g, h, g * SSM_HEADS_PER_GROUP + h] = 1.0
    return jnp.asarray(e, BF16), jnp.asarray(st, BF16)


def _pad_lanes(v):
    return jnp.pad(v.astype(F32), (0, LANES - v.shape[0])).reshape(1, LANES)


def kernel(x, ffn1_norm, ffn1_w_gate, ffn1_w_up, ffn1_w_down, mix_norm, w_in, conv_w, conv_b, dt_bias, a_log, d_skip, ssm_norm, q_norm, k_norm, sinks, w_o_ssm, w_o_attn, w_out, ffn2_norm, ffn2_w_gate, ffn2_w_up, ffn2_w_down):
    bsz, t_len, d = x.shape
    m = bsz * t_len
    depth = ffn1_norm.shape[0]
    e01, selt = _selection_constants()
    xf = x.reshape(m, d)
    for l in range(depth):
        wit = jnp.swapaxes(w_in[l], 0, 1)

        xf = _ffn(xf, ffn1_norm[l].reshape(1, d), ffn1_w_gate[l].astype(BF16),
                  ffn1_w_up[l].astype(BF16), ffn1_w_down[l].astype(BF16), tm=1024, tf=512)

        pa, dt_raw = _inproj_ssm(xf, mix_norm[l].reshape(1, d), wit, tm=1024, tn=1280)
        pb = _inproj_attn(xf, mix_norm[l].reshape(1, d), wit, tm=1024, tn=1664)

        y_ssm, (wg2, wu2, wd2) = _ssd(
            pa, dt_raw, conv_w[l], conv_b[l].reshape(1, -1),
            _pad_lanes(dt_bias[l]), _pad_lanes(a_log[l]),
            jnp.repeat(d_skip[l].astype(F32), SSM_HEAD_DIM).reshape(1, SSM_D_INNER),
            ssm_norm[l].reshape(1, SSM_D_INNER), e01, selt,
            (ffn2_w_gate[l], ffn2_w_up[l], ffn2_w_down[l]), bsz, t_len)

        y_attn, (wos, woa) = _attn(
            pb, sinks[l].astype(F32),
            jnp.tile(q_norm[l].astype(F32), LANES // ATTN_HEAD_DIM).reshape(1, LANES),
            jnp.tile(k_norm[l].astype(F32), LANES // ATTN_HEAD_DIM).reshape(1, LANES),
            (w_o_ssm[l], w_o_attn[l]), bsz, t_len)

        mg = _merge(y_ssm, y_attn, pb, wos, woa, tm=1024, tn=512)
        xf = _outproj(xf, mg, w_out[l], tm=512)

        xf = _ffn(xf, ffn2_norm[l].reshape(1, d), wg2, wu2, wd2, tm=1024, tf=512)
    return xf.reshape(bsz, t_len, d)
```

```python
import functools

import jax
import jax.numpy as jnp
import numpy as np
from jax import lax
from jax.experimental import pallas as pl
from jax.experimental.pallas import tpu as pltpu

F32 = jnp.float32
BF16 = jnp.bfloat16

D_MODEL = 2048
SSM_D_INNER = 4096
SSM_HEAD_DIM = 64
SSM_N_HEADS = 64
SSM_N_GROUPS = 8
SSM_HEADS_PER_GROUP = SSM_N_HEADS // SSM_N_GROUPS
SSM_GROUP_WIDTH = SSM_D_INNER // SSM_N_GROUPS
SSM_D_STATE = 128
SSM_CONV = 4
SSM_CHUNK = 256
SSM_GN = SSM_N_GROUPS * SSM_D_STATE
ATTN_HEAD_DIM = 64
ATTN_N_HEADS = 32
ATTN_N_KV = 4
ATTN_Q_PER_KV = 8
ATTN_D = ATTN_N_HEADS * ATTN_HEAD_DIM
ATTN_KV_D = ATTN_N_KV * ATTN_HEAD_DIM
WINDOW = 128
ATTN_BLOCK = 128
D_FF = 5632
EPS = 1e-6
NEG = -1e30
LOG2E = 1.4426950408889634

LANES = 128
CONV_HALO = 8

PA_Z = 0
PA_X = PA_Z + SSM_D_INNER
PA_B = PA_X + SSM_D_INNER
PA_C = PA_B + SSM_GN
PA_COLS = PA_C + SSM_GN
PB_Q = 0
PB_K = PB_Q + ATTN_D
PB_V = PB_K + ATTN_KV_D
PB_GS = PB_V + ATTN_KV_D
PB_GA = PB_GS + D_MODEL
PB_COLS = PB_GA + D_MODEL

W_DT = PA_COLS
W_Q = W_DT + SSM_N_HEADS

VMEM_LIMIT = 56 * 1024 * 1024

TM = 1024
FFN_TF = 512
INPROJ_SSM_TN = 1280
INPROJ_ATTN_TN = 1664
MERGE_TN = 512
OUTPROJ_TM = 512


def _params(sem):
    return pltpu.CompilerParams(dimension_semantics=sem, vmem_limit_bytes=VMEM_LIMIT)


def _rms(x, gain):
    return x * lax.rsqrt(jnp.mean(x * x, axis=-1, keepdims=True) + EPS) * gain


def _silu(x):
    return x * jax.nn.sigmoid(x)


def _dot(a, b):
    return jnp.dot(a, b, preferred_element_type=F32)


def _dot_nt(a, b):
    return lax.dot_general(a, b, (((1,), (1,)), ((), ())), preferred_element_type=F32)


def _split3(x):
    hi = x.astype(BF16)
    r1 = x - hi.astype(F32)
    mid = r1.astype(BF16)
    lo = (r1 - mid.astype(F32)).astype(BF16)
    return hi, mid, lo


def _sel_left(m01, x):
    hi, mid, lo = _split3(x)
    return (_dot(m01, lo) + _dot(m01, mid)) + _dot(m01, hi)


def _sel_right(x, m01):
    hi, mid, lo = _split3(x)
    return (_dot(lo, m01) + _dot(mid, m01)) + _dot(hi, m01)


def _ffn_kernel(x_ref, gain_ref, wg_ref, wu_ref, wd_ref, o_ref, h_ref):
    j = pl.program_id(1)

    @pl.when(j == 0)
    def _():
        x = x_ref[...]
        h_ref[...] = _rms(x, gain_ref[...]).astype(BF16)
        o_ref[...] = x

    h = h_ref[...]
    g = _dot(h, wg_ref[...])
    u = _dot(h, wu_ref[...])
    a = (0.5 * _silu(g) * u).astype(BF16)
    o_ref[...] += _dot(a, wd_ref[...])


def _ffn(x, gain, wg, wu, wd, tm, tf):
    m, d = x.shape
    dff = wg.shape[1]
    return pl.pallas_call(
        _ffn_kernel,
        grid=(m // tm, dff // tf),
        in_specs=[
            pl.BlockSpec((tm, d), lambda i, j: (i, 0)),
            pl.BlockSpec((1, d), lambda i, j: (0, 0)),
            pl.BlockSpec((d, tf), lambda i, j: (0, j)),
            pl.BlockSpec((d, tf), lambda i, j: (0, j)),
            pl.BlockSpec((tf, d), lambda i, j: (j, 0)),
        ],
        out_specs=pl.BlockSpec((tm, d), lambda i, j: (i, 0)),
        out_shape=jax.ShapeDtypeStruct((m, d), F32),
        scratch_shapes=[pltpu.VMEM((tm, d), BF16)],
        compiler_params=_params(("parallel", "arbitrary")),
        name="ffn",
    )(x, gain, wg, wu, wd)


def _inproj_ssm_kernel(x_ref, gain_ref, w_ref, wdt_ref, p_ref, dt_ref, h_ref):
    j = pl.program_id(1)

    @pl.when(j == 0)
    def _():
        h = _rms(x_ref[...], gain_ref[...]).astype(BF16)
        h_ref[...] = h
        dt = _dot_nt(h, wdt_ref[...].astype(BF16))
        lane = lax.broadcasted_iota(jnp.int32, dt.shape, 1)
        dt_ref[...] = jnp.where(lane < SSM_N_HEADS, dt, 0.0)

    p_ref[...] = _dot_nt(h_ref[...], w_ref[...].astype(BF16)).astype(BF16)


def _inproj_attn_kernel(x_hbm, gain_ref, w_ref, p_ref, h_ref, xbuf, sem):
    i = pl.program_id(0)
    j = pl.program_id(1)
    tm = xbuf.shape[0]

    def x_copy(tile):
        return pltpu.make_async_copy(x_hbm.at[pl.ds(tile * tm, tm), :], xbuf, sem)

    @pl.when((i == 0) & (j == 0))
    def _():
        x_copy(0).start()

    @pl.when(j == 0)
    def _():
        x_copy(i).wait()
        h_ref[...] = _rms(xbuf[...], gain_ref[...]).astype(BF16)

    @pl.when((j == 1) & (i + 1 < pl.num_programs(0)))
    def _():
        x_copy(i + 1).start()

    p_ref[...] = _dot_nt(h_ref[...], w_ref[...].astype(BF16)).astype(BF16)


def _inproj_ssm(x, gain, wit, tm, tn):
    m, d = x.shape
    return pl.pallas_call(
        _inproj_ssm_kernel,
        grid=(m // tm, PA_COLS // tn),
        in_specs=[
            pl.BlockSpec((tm, d), lambda i, j: (i, 0)),
            pl.BlockSpec((1, d), lambda i, j: (0, 0)),
            pl.BlockSpec((tn, d), lambda i, j: (j, 0)),
            pl.BlockSpec((LANES, d), lambda i, j: (W_DT // LANES, 0)),
        ],
        out_specs=[
            pl.BlockSpec((tm, tn), lambda i, j: (i, j)),
            pl.BlockSpec((tm, LANES), lambda i, j: (i, 0)),
        ],
        out_shape=[
            jax.ShapeDtypeStruct((m, PA_COLS), BF16),
            jax.ShapeDtypeStruct((m, LANES), F32),
        ],
        scratch_shapes=[pltpu.VMEM((tm, d), BF16)],
        compiler_params=_params(("parallel", "arbitrary")),
        name="in_proj_ssm",
    )(x, gain, wit, wit)


def _inproj_attn(x, gain, wit, tm, tn):
    m, d = x.shape
    return pl.pallas_call(
        _inproj_attn_kernel,
        grid=(m // tm, PB_COLS // tn),
        in_specs=[
            pl.BlockSpec(memory_space=pl.ANY),
            pl.BlockSpec((1, d), lambda i, j: (0, 0)),
            pl.BlockSpec((pl.Element(tn), pl.Element(d)), lambda i, j: (pl.multiple_of(W_Q + j * tn, SSM_N_HEADS), 0)),
        ],
        out_specs=pl.BlockSpec((tm, tn), lambda i, j: (i, j)),
        out_shape=jax.ShapeDtypeStruct((m, PB_COLS), BF16),
        scratch_shapes=[pltpu.VMEM((tm, d), BF16), pltpu.VMEM((tm, d), F32), pltpu.SemaphoreType.DMA(())],
        compiler_params=_params(("arbitrary", "arbitrary")),
        name="in_proj_attn",
    )(x, gain, wit)


def _ssd_kernel(z_ref, x_ref, b_ref, c_ref, dt_ref,
                cwx_ref, cwb_ref, cwc_ref, cbx_ref, cbb_ref, cbc_ref,
                dtb_ref, alog_ref, dsk_ref, nrm_ref, e_ref, selt_ref,
                o_ref, ext_ref, state_ref):
    c = pl.program_id(2)
    L = SSM_CHUNK
    GW = SSM_GROUP_WIDTH
    N = SSM_D_STATE
    XBC = GW + 2 * N

    @pl.when(c == 0)
    def _():
        ext_ref[0:CONV_HALO, :] = jnp.zeros((CONV_HALO, XBC), F32)
        state_ref[...] = jnp.zeros_like(state_ref)

    ext_ref[CONV_HALO:CONV_HALO + L, 0:GW] = x_ref[...].astype(F32)
    ext_ref[CONV_HALO:CONV_HALO + L, GW:GW + N] = b_ref[...].astype(F32)
    ext_ref[CONV_HALO:CONV_HALO + L, GW + N:XBC] = c_ref[...].astype(F32)

    cw = jnp.concatenate([cwx_ref[...], cwb_ref[...], cwc_ref[...]], axis=1)
    cbias = jnp.concatenate([cbx_ref[...], cbb_ref[...], cbc_ref[...]], axis=1)
    ext = ext_ref[...]
    acc = cw[0:1, :] * ext
    for k in range(1, SSM_CONV):
        acc = pltpu.roll(acc, 1, axis=0) + cw[k:k + 1, :] * ext
    ext_ref[0:CONV_HALO, :] = ext_ref[L:L + CONV_HALO, :]
    xbc = _silu(acc[CONV_HALO:, :] + cbias)
    xs = xbc[:, 0:GW]
    bm = xbc[:, GW:GW + N]
    cm = xbc[:, GW + N:XBC]

    dtr = dt_ref[...] + dtb_ref[...]
    dt = jnp.maximum(dtr, 0.0) + jnp.log1p(jnp.exp(-jnp.abs(dtr)))
    e01 = e_ref[...]
    dt_x = _sel_right(dt, e01)
    adt_c = dt * (-LOG2E * jnp.exp(alog_ref[...]))
    row = lax.broadcasted_iota(jnp.int32, (L, L), 0)
    col = lax.broadcasted_iota(jnp.int32, (L, L), 1)
    tril = row >= col
    tril01 = tril.astype(BF16)
    acum_c = _sel_left(tril01, adt_c)
    selt = selt_ref[...]
    hi, mid, lo = _split3(acum_c)
    acum_r = (_dot_nt(selt, lo) + _dot_nt(selt, mid)) + _dot_nt(selt, hi)
    acum_x = (_dot(lo, e01) + _dot(mid, e01)) + _dot(hi, e01)

    bt = bm.T.astype(BF16)
    cmb = cm.astype(BF16)
    H = L // 2
    cb_top = _dot(cmb[0:H, :], bt[:, 0:H])
    cb_bot = _dot(cmb[H:L, :], bt)
    xdt = xs * dt_x
    xdt_b = xdt.astype(BF16)

    tri = tril[0:H, 0:H]
    lane = lax.broadcasted_iota(jnp.int32, (L, LANES), 1)
    low_half = lane < SSM_HEAD_DIM
    pairs = []
    for pair in range(SSM_HEADS_PER_GROUP // 2):
        xp = xdt_b[:, pair * LANES:(pair + 1) * LANES]
        y_top = None
        y_bot = None
        for half in range(2):
            h = 2 * pair + half
            a_col = acum_x[:, h * SSM_HEAD_DIM:h * SSM_HEAD_DIM + 1]
            a_row = acum_r[h:h + 1, :]
            w00 = cb_top * jnp.exp2(jnp.where(tri, a_col[0:H] - a_row[:, 0:H], NEG))
            w10 = cb_bot[:, 0:H] * jnp.exp2(a_col[H:L] - a_row[:, 0:H])
            w11 = cb_bot[:, H:L] * jnp.exp2(jnp.where(tri, a_col[H:L] - a_row[:, H:L], NEG))
            keep = low_half if half == 0 else jnp.logical_not(low_half)
            xh = jnp.where(keep, xp, jnp.zeros_like(xp))
            yt = _dot(w00.astype(BF16), xh[0:H, :])
            yb = _dot(jnp.concatenate([w10, w11], axis=1).astype(BF16), xh)
            y_top = yt if y_top is None else y_top + yt
            y_bot = yb if y_bot is None else y_bot + yb
        pairs.append(jnp.concatenate([y_top, y_bot], axis=0))
    y_diag = jnp.concatenate(pairs, axis=1)

    state = state_ref[...]
    y_off = jnp.exp2(acum_x) * _dot(cmb, state.astype(BF16))
    y = y_diag + y_off + dsk_ref[...] * xs

    last = acum_x[L - 1:L, :]
    xdec = (xdt * jnp.exp2(last - acum_x)).astype(BF16)
    state_ref[...] = state * jnp.exp2(last) + _dot(bt, xdec)

    yg = y * _silu(z_ref[...].astype(F32))
    o_ref[...] = _rms(yg, nrm_ref[...]).astype(BF16)


def _ssd(p, dt_raw, conv_w, conv_b, dtb_row, alog_row, dskip_x, nrm_row, e01, selt, bsz, t_len):
    L = SSM_CHUNK
    nc = t_len // L
    GW = SSM_GROUP_WIDTH
    N = SSM_D_STATE
    rowblk = lambda b, g, c: b * nc + c
    return pl.pallas_call(
        _ssd_kernel,
        grid=(bsz, SSM_N_GROUPS, nc),
        in_specs=[
            pl.BlockSpec((L, GW), lambda b, g, c: (rowblk(b, g, c), PA_Z // GW + g)),
            pl.BlockSpec((L, GW), lambda b, g, c: (rowblk(b, g, c), PA_X // GW + g)),
            pl.BlockSpec((L, N), lambda b, g, c: (rowblk(b, g, c), PA_B // N + g)),
            pl.BlockSpec((L, N), lambda b, g, c: (rowblk(b, g, c), PA_C // N + g)),
            pl.BlockSpec((L, LANES), lambda b, g, c: (rowblk(b, g, c), 0)),
            pl.BlockSpec((SSM_CONV, GW), lambda b, g, c: (0, g)),
            pl.BlockSpec((SSM_CONV, N), lambda b, g, c: (0, SSM_D_INNER // N + g)),
            pl.BlockSpec((SSM_CONV, N), lambda b, g, c: (0, (SSM_D_INNER + SSM_GN) // N + g)),
            pl.BlockSpec((1, GW), lambda b, g, c: (0, g)),
            pl.BlockSpec((1, N), lambda b, g, c: (0, SSM_D_INNER // N + g)),
            pl.BlockSpec((1, N), lambda b, g, c: (0, (SSM_D_INNER + SSM_GN) // N + g)),
            pl.BlockSpec((1, LANES), lambda b, g, c: (0, 0)),
            pl.BlockSpec((1, LANES), lambda b, g, c: (0, 0)),
            pl.BlockSpec((1, GW), lambda b, g, c: (0, g)),
            pl.BlockSpec((1, GW), lambda b, g, c: (0, g)),
            pl.BlockSpec((None, LANES, GW), lambda b, g, c: (g, 0, 0)),
            pl.BlockSpec((None, SSM_HEADS_PER_GROUP, LANES), lambda b, g, c: (g, 0, 0)),
        ],
        out_specs=pl.BlockSpec((L, GW), lambda b, g, c: (rowblk(b, g, c), g)),
        out_shape=jax.ShapeDtypeStruct((bsz * t_len, SSM_D_INNER), BF16),
        scratch_shapes=[
            pltpu.VMEM((CONV_HALO + L, GW + 2 * N), F32),
            pltpu.VMEM((N, GW), F32),
        ],
        compiler_params=_params(("parallel", "parallel", "arbitrary")),
        name="ssd",
    )(p, p, p, p, dt_raw, conv_w, conv_w, conv_w, conv_b, conv_b, conv_b,
      dtb_row, alog_row, dskip_x, nrm_row, e01, selt)


def _half_rms_scale(x, low):
    sq = x * x
    zero = jnp.zeros_like(sq)
    ss_lo = jnp.sum(jnp.where(low, sq, zero), axis=-1, keepdims=True)
    ss_hi = jnp.sum(jnp.where(low, zero, sq), axis=-1, keepdims=True)
    inv = 1.0 / ATTN_HEAD_DIM
    return jnp.where(low, lax.rsqrt(ss_lo * inv + EPS), lax.rsqrt(ss_hi * inv + EPS))


RIDER_ROWS = 16


def _rider_span(rows, n_steps, max_span):
    span = 1
    while (rows * span) % (n_steps * RIDER_ROWS) != 0:
        span *= 2
        assert span <= max_span, (rows, n_steps)
    return span


def _attn_kernel(sink_ref, slope_ref, q_ref, kc_ref, kp_ref, vc_ref, vp_ref, qn_ref, kn_ref, *refs, rider_spans):
    nr = len(rider_spans)
    r_refs, o_ref, q_refs = refs[:nr], refs[nr], refs[nr + 1:]
    n = pl.program_id(1)
    blk = ATTN_BLOCK
    hd = ATTN_HEAD_DIM
    kf = jnp.concatenate([kp_ref[...], kc_ref[...]], axis=0).astype(F32)
    vf = jnp.concatenate([vp_ref[...], vc_ref[...]], axis=0).astype(F32)
    sj = lax.broadcasted_iota(jnp.int32, (2 * blk, blk), 0)
    qi = lax.broadcasted_iota(jnp.int32, (2 * blk, blk), 1)
    dist = qi + blk - sj
    valid = (dist >= 0) & (dist < WINDOW) & ((sj >= blk) | (n > 0))
    ndm = jnp.where(valid, -dist.astype(F32), NEG)
    npair = ATTN_Q_PER_KV // 2
    ndm4 = jnp.concatenate([ndm] * npair, axis=1)
    low_k = lax.broadcasted_iota(jnp.int32, (2 * blk, LANES), 1) < hd
    low_q = lax.broadcasted_iota(jnp.int32, (npair * blk, LANES), 1) < hd
    qn = qn_ref[...] * (LOG2E * ATTN_HEAD_DIM ** -0.5)
    kn = kn_ref[...]
    zk = jnp.zeros((2 * blk, LANES), F32)
    for kvp in range(ATTN_N_KV // 2):
        k2 = kf[:, kvp * LANES:(kvp + 1) * LANES]
        k2 = k2 * _half_rms_scale(k2, low_k) * kn
        v2 = vf[:, kvp * LANES:(kvp + 1) * LANES]
        k2r = pltpu.roll(k2, hd, axis=1)
        v2r = pltpu.roll(v2, hd, axis=1)
        for half in range(2):
            kv = 2 * kvp + half
            k_lo_src, k_hi_src = (k2, k2r) if half == 0 else (k2r, k2)
            v_lo_src, v_hi_src = (v2, v2r) if half == 0 else (v2r, v2)
            k_lo = jnp.where(low_k, k_lo_src, zk).astype(BF16)
            k_hi = jnp.where(low_k, zk, k_hi_src).astype(BF16)
            v_lo = jnp.where(low_k, v_lo_src, zk).astype(BF16)
            v_hi = jnp.where(low_k, zk, v_hi_src).astype(BF16)
            q4 = jnp.concatenate(
                [q_ref[:, (kv * npair + j) * LANES:(kv * npair + j + 1) * LANES] for j in range(npair)],
                axis=0).astype(F32)
            qb = (q4 * _half_rms_scale(q4, low_q) * qn).astype(BF16)
            o4 = None
            for e in range(2):
                u = 2 * kv + e
                st = _dot_nt(k_lo if e == 0 else k_hi, qb) + slope_ref[u:u + 1, :] * ndm4
                sink = sink_ref[u:u + 1, :] * LOG2E
                m = jnp.maximum(jnp.max(st, axis=0, keepdims=True), sink)
                p = jnp.exp2(st - m)
                denom = jnp.sum(p, axis=0, keepdims=True) + jnp.exp2(sink - m)
                pn = (p * (1.0 / denom)).astype(BF16)
                oe = lax.dot_general(pn, v_lo if e == 0 else v_hi, (((0,), (0,)), ((), ())),
                                     preferred_element_type=F32)
                o4 = oe if o4 is None else o4 + oe
            for j in range(npair):
                hp = kv * npair + j
                o_ref[:, hp * LANES:(hp + 1) * LANES] = o4[j * blk:(j + 1) * blk, :].astype(BF16)

    for r_ref, q_ref, span in zip(r_refs, q_refs, rider_spans):
        if span == 1:
            q_ref[...] = r_ref[...].astype(BF16)
        else:
            @pl.when(n % span == 0)
            def _(r_ref=r_ref, q_ref=q_ref):
                q_ref[...] = r_ref[...].astype(BF16)


def _head_table(per_head):
    npair = ATTN_Q_PER_KV // 2
    t = per_head.astype(F32).reshape(ATTN_N_KV, npair, 2).transpose(0, 2, 1)
    return jnp.repeat(t.reshape(2 * ATTN_N_KV, npair), LANES, axis=1)


def _attn(p, sinks, qn_row, kn_row, riders, bsz, t_len):
    blk = ATTN_BLOCK
    slopes = jnp.asarray(
        [2.0 ** (-8.0 * (h + 1) / ATTN_N_HEADS) for h in range(ATTN_N_HEADS)], F32)
    nb = t_len // blk
    cur = lambda b, n: b * nb + n
    prev = lambda b, n: b * nb + jnp.maximum(n - 1, 0)
    n_steps = bsz * nb
    spans = tuple(_rider_span(r.shape[0], n_steps, nb) for r in riders)
    rider_specs = [
        pl.BlockSpec((r.shape[0] * span // n_steps, r.shape[1]), lambda b, n, span=span: (cur(b, n) // span, 0))
        for r, span in zip(riders, spans)]
    outs = pl.pallas_call(
        functools.partial(_attn_kernel, rider_spans=spans),
        grid=(bsz, nb),
        in_specs=[
            pl.BlockSpec((2 * ATTN_N_KV, 4 * LANES), lambda b, n: (0, 0)),
            pl.BlockSpec((2 * ATTN_N_KV, 4 * LANES), lambda b, n: (0, 0)),
            pl.BlockSpec((blk, ATTN_D), lambda b, n: (cur(b, n), PB_Q // ATTN_D)),
            pl.BlockSpec((blk, ATTN_KV_D), lambda b, n: (cur(b, n), PB_K // ATTN_KV_D)),
            pl.BlockSpec((blk, ATTN_KV_D), lambda b, n: (prev(b, n), PB_K // ATTN_KV_D)),
            pl.BlockSpec((blk, ATTN_KV_D), lambda b, n: (cur(b, n), PB_V // ATTN_KV_D)),
            pl.BlockSpec((blk, ATTN_KV_D), lambda b, n: (prev(b, n), PB_V // ATTN_KV_D)),
            pl.BlockSpec((1, LANES), lambda b, n: (0, 0)),
            pl.BlockSpec((1, LANES), lambda b, n: (0, 0)),
        ] + rider_specs,
        out_specs=[pl.BlockSpec((blk, ATTN_D), lambda b, n: (cur(b, n), 0))] + rider_specs,
        out_shape=[jax.ShapeDtypeStruct((bsz * t_len, ATTN_D), BF16)]
        + [jax.ShapeDtypeStruct(r.shape, BF16) for r in riders],
        compiler_params=_params(("parallel", "arbitrary")),
        name="swa",
    )(_head_table(sinks), _head_table(slopes * LOG2E), p, p, p, p, p, qn_row, kn_row, *riders)
    return outs[0], outs[1:]


def _merge_kernel(ys_ref, ya_ref, gs_ref, ga_ref, wos_ref, woa_ref, o_ref):
    ms = _dot(ys_ref[...], wos_ref[...])
    ma = _dot(ya_ref[...], woa_ref[...])
    mg = (jax.nn.sigmoid(gs_ref[...].astype(F32)) * ms
          + jax.nn.sigmoid(ga_ref[...].astype(F32)) * ma)
    o_ref[...] = mg.astype(BF16)


def _merge(ys, ya, p, wos, woa, tm, tn):
    m = ys.shape[0]
    d = wos.shape[1]
    return pl.pallas_call(
        _merge_kernel,
        grid=(m // tm, d // tn),
        in_specs=[
            pl.BlockSpec((tm, SSM_D_INNER), lambda i, j: (i, 0)),
            pl.BlockSpec((tm, ATTN_D), lambda i, j: (i, 0)),
            pl.BlockSpec((tm, tn), lambda i, j: (i, PB_GS // tn + j)),
            pl.BlockSpec((tm, tn), lambda i, j: (i, PB_GA // tn + j)),
            pl.BlockSpec((SSM_D_INNER, tn), lambda i, j: (0, j)),
            pl.BlockSpec((ATTN_D, tn), lambda i, j: (0, j)),
        ],
        out_specs=pl.BlockSpec((tm, tn), lambda i, j: (i, j)),
        out_shape=jax.ShapeDtypeStruct((m, d), BF16),
        compiler_params=_params(("parallel", "arbitrary")),
        name="merge",
    )(ys, ya, p, p, wos, woa)


def _outproj_kernel(x_ref, m_ref, w_ref, o_ref, wb_ref):
    @pl.when(pl.program_id(0) == 0)
    def _():
        wb_ref[...] = w_ref[...].astype(BF16)

    o_ref[...] = x_ref[...] + _dot(m_ref[...], wb_ref[...])


def _outproj(x, mg, wout, tm):
    m, d = x.shape
    return pl.pallas_call(
        _outproj_kernel,
        grid=(m // tm,),
        in_specs=[
            pl.BlockSpec((tm, d), lambda i: (i, 0)),
            pl.BlockSpec((tm, d), lambda i: (i, 0)),
            pl.BlockSpec((d, d), lambda i: (0, 0), pipeline_mode=pl.Buffered(1)),
        ],
        out_specs=pl.BlockSpec((tm, d), lambda i: (i, 0)),
        out_shape=jax.ShapeDtypeStruct((m, d), F32),
        scratch_shapes=[pltpu.VMEM((d, d), BF16)],
        compiler_params=_params(("arbitrary",)),
        name="out_proj",
    )(x, mg, wout)


def _selection_constants():
    e = np.zeros((SSM_N_GROUPS, LANES, SSM_GROUP_WIDTH), np.float32)
    st = np.zeros((SSM_N_GROUPS, SSM_HEADS_PER_GROUP, LANES), np.float32)
    for g in range(SSM_N_GROUPS):
        for h in range(SSM_HEADS_PER_GROUP):
            e[g, g * SSM_HEADS_PER_GROUP + h, h * SSM_HEAD_DIM:(h + 1) * SSM_HEAD_DIM] = 1.0
            st[g, h, g * SSM_HEADS_PER_GROUP + h] = 1.0
    return jnp.asarray(e, BF16), jnp.asarray(st, BF16)


def _pad_lanes(v):
    return jnp.pad(v.astype(F32), (0, LANES - v.shape[0])).reshape(1, LANES)


def kernel(x, ffn1_norm, ffn1_w_gate, ffn1_w_up, ffn1_w_down, mix_norm, w_in, conv_w, conv_b, dt_bias, a_log, d_skip, ssm_norm, q_norm, k_norm, sinks, w_o_ssm, w_o_attn, w_out, ffn2_norm, ffn2_w_gate, ffn2_w_up, ffn2_w_down):
    bsz, t_len, d = x.shape
    m = bsz * t_len
    depth = ffn1_norm.shape[0]
    e01, selt = _selection_constants()
    xf = x.reshape(m, d)
    for l in range(depth):
        wit = jnp.swapaxes(w_in[l], 0, 1)

        xf = _ffn(xf, ffn1_norm[l].reshape(1, d), ffn1_w_gate[l].astype(BF16),
                  ffn1_w_up[l].astype(BF16), ffn1_w_down[l].astype(BF16), tm=TM, tf=FFN_TF)

        pa, dt_raw = _inproj_ssm(xf, mix_norm[l].reshape(1, d), wit, tm=TM, tn=INPROJ_SSM_TN)
        pb = _inproj_attn(xf, mix_norm[l].reshape(1, d), wit, tm=TM, tn=INPROJ_ATTN_TN)

        y_ssm = _ssd(
            pa, dt_raw, conv_w[l], conv_b[l].reshape(1, -1),
            _pad_lanes(dt_bias[l]), _pad_lanes(a_log[l]),
            jnp.repeat(d_skip[l].astype(F32), SSM_HEAD_DIM).reshape(1, SSM_D_INNER),
            ssm_norm[l].reshape(1, SSM_D_INNER), e01, selt, bsz, t_len)

        y_attn, (wos, woa, wg2, wu2, wd2) = _attn(
            pb, sinks[l].astype(F32),
            jnp.tile(q_norm[l].astype(F32), LANES // ATTN_HEAD_DIM).reshape(1, LANES),
            jnp.tile(k_norm[l].astype(F32), LANES // ATTN_HEAD_DIM).reshape(1, LANES),
            (w_o_ssm[l], w_o_attn[l], ffn2_w_gate[l], ffn2_w_up[l], ffn2_w_down[l]), bsz, t_len)

        mg = _merge(y_ssm, y_attn, pb, wos, woa, tm=TM, tn=MERGE_TN)
        xf = _outproj(xf, mg, w_out[l], tm=OUTPROJ_TM)

        xf = _ffn(xf, ffn2_norm[l].reshape(1, d), wg2, wu2, wd2, tm=TM, tf=FFN_TF)
    return xf.reshape(bsz, t_len, d)
```

```python
import functools

import jax
import jax.numpy as jnp
import numpy as np
from jax import lax
from jax.experimental import pallas as pl
from jax.experimental.pallas import tpu as pltpu

F32 = jnp.float32
BF16 = jnp.bfloat16

D_MODEL = 2048
SSM_D_INNER = 4096
SSM_HEAD_DIM = 64
SSM_N_HEADS = 64
SSM_N_GROUPS = 8
SSM_HEADS_PER_GROUP = SSM_N_HEADS // SSM_N_GROUPS
SSM_GROUP_WIDTH = SSM_D_INNER // SSM_N_GROUPS
SSM_D_STATE = 128
SSM_CONV = 4
SSM_CHUNK = 256
SSM_GN = SSM_N_GROUPS * SSM_D_STATE
ATTN_HEAD_DIM = 64
ATTN_N_HEADS = 32
ATTN_N_KV = 4
ATTN_Q_PER_KV = 8
ATTN_D = ATTN_N_HEADS * ATTN_HEAD_DIM
ATTN_KV_D = ATTN_N_KV * ATTN_HEAD_DIM
WINDOW = 128
ATTN_BLOCK = 128
D_FF = 5632
EPS = 1e-6
NEG = -1e30
LOG2E = 1.4426950408889634

LANES = 128
CONV_HALO = 8

PA_Z = 0
PA_X = PA_Z + SSM_D_INNER
PA_B = PA_X + SSM_D_INNER
PA_C = PA_B + SSM_GN
PA_COLS = PA_C + SSM_GN
PB_Q = PA_COLS
PB_K = PB_Q + ATTN_D
PB_V = PB_K + ATTN_KV_D
PB_GS = PB_V + ATTN_KV_D
PB_GA = PB_GS + D_MODEL
P_COLS = PB_GA + D_MODEL

W_DT = PA_COLS
W_Q = W_DT + SSM_N_HEADS

VMEM_LIMIT = 56 * 1024 * 1024

TM = 1024
FFN_TF = 512
INPROJ_TN = 1536
MERGE_TN = 512
OUTPROJ_TM = 512


def _params(sem):
    return pltpu.CompilerParams(dimension_semantics=sem, vmem_limit_bytes=VMEM_LIMIT)


def _rms(x, gain):
    return x * lax.rsqrt(jnp.mean(x * x, axis=-1, keepdims=True) + EPS) * gain


def _silu(x):
    return x * jax.nn.sigmoid(x)


def _dot(a, b):
    return jnp.dot(a, b, preferred_element_type=F32)


def _dot_nt(a, b):
    return lax.dot_general(a, b, (((1,), (1,)), ((), ())), preferred_element_type=F32)


def _split3(x):
    hi = x.astype(BF16)
    r1 = x - hi.astype(F32)
    mid = r1.astype(BF16)
    lo = (r1 - mid.astype(F32)).astype(BF16)
    return hi, mid, lo


def _sel_left(m01, x):
    hi, mid, lo = _split3(x)
    return (_dot(m01, lo) + _dot(m01, mid)) + _dot(m01, hi)


def _sel_right(x, m01):
    hi, mid, lo = _split3(x)
    return (_dot(lo, m01) + _dot(mid, m01)) + _dot(hi, m01)


PACK_ROWS = 256


def _ffn_kernel(x_ref, gain_ref, wg_ref, wu_ref, wd_ref, *refs, n_pack):
    if n_pack:
        wi_ref, o_ref, wp_ref, h_ref = refs
    else:
        o_ref, h_ref = refs
    i = pl.program_id(0)
    j = pl.program_id(1)

    @pl.when(j == 0)
    def _():
        x = x_ref[...]
        h_ref[...] = _rms(x, gain_ref[...]).astype(BF16)
        o_ref[...] = x

    h = h_ref[...]
    g = _dot(h, wg_ref[...])
    u = _dot(h, wu_ref[...])
    a = (0.5 * _silu(g) * u).astype(BF16)
    o_ref[...] += _dot(a, wd_ref[...])

    if n_pack:
        @pl.when(i * pl.num_programs(1) + j < n_pack)
        def _():
            wp_ref[...] = wi_ref[...].astype(BF16)


def _ffn(x, gain, wg, wu, wd, tm, tf, wit=None):
    m, d = x.shape
    dff = wg.shape[1]
    nj = dff // tf
    in_specs = [
        pl.BlockSpec((tm, d), lambda i, j: (i, 0)),
        pl.BlockSpec((1, d), lambda i, j: (0, 0)),
        pl.BlockSpec((d, tf), lambda i, j: (0, j)),
        pl.BlockSpec((d, tf), lambda i, j: (0, j)),
        pl.BlockSpec((tf, d), lambda i, j: (j, 0)),
    ]
    out_specs = [pl.BlockSpec((tm, d), lambda i, j: (i, 0))]
    out_shape = [jax.ShapeDtypeStruct((m, d), F32)]
    operands = [x, gain, wg, wu, wd]
    n_pack = 0
    if wit is not None:
        n_pack = P_COLS // PACK_ROWS
        assert n_pack <= (m // tm) * nj and PA_COLS % PACK_ROWS == 0
        blk = lambda i, j: jnp.minimum(i * nj + j, n_pack - 1)
        in_specs[0] = pl.BlockSpec((tm, d), lambda i, j: (i, 0), pipeline_mode=pl.Buffered(1))
        in_specs.append(pl.BlockSpec(
            (pl.Element(PACK_ROWS), pl.Element(d)),
            lambda i, j: (pl.multiple_of(
                blk(i, j) * PACK_ROWS + jnp.where(blk(i, j) * PACK_ROWS >= PA_COLS, SSM_N_HEADS, 0),
                SSM_N_HEADS), 0)))
        out_specs.append(pl.BlockSpec((PACK_ROWS, d), lambda i, j: (blk(i, j), 0)))
        out_shape.append(jax.ShapeDtypeStruct((P_COLS, d), BF16))
        operands.append(wit)
    outs = pl.pallas_call(
        functools.partial(_ffn_kernel, n_pack=n_pack),
        grid=(m // tm, nj),
        in_specs=in_specs,
        out_specs=out_specs,
        out_shape=out_shape,
        scratch_shapes=[pltpu.VMEM((tm, d), BF16)],
        compiler_params=_params(("arbitrary" if n_pack else "parallel", "arbitrary")),
        name="ffn",
    )(*operands)
    return outs if n_pack else outs[0]


def _inproj_kernel(x_ref, gain_ref, w_ref, wdt_ref, p_ref, dt_ref, h_ref):
    j = pl.program_id(1)

    @pl.when(j == 0)
    def _():
        h = _rms(x_ref[...], gain_ref[...]).astype(BF16)
        h_ref[...] = h
        dt = _dot_nt(h, wdt_ref[...].astype(BF16))
        lane = lax.broadcasted_iota(jnp.int32, dt.shape, 1)
        dt_ref[...] = jnp.where(lane < SSM_N_HEADS, dt, 0.0)

    p_ref[...] = _dot_nt(h_ref[...], w_ref[...]).astype(BF16)


def _inproj(x, gain, wpt, wit, tm, tn):
    m, d = x.shape
    return pl.pallas_call(
        _inproj_kernel,
        grid=(m // tm, P_COLS // tn),
        in_specs=[
            pl.BlockSpec((tm, d), lambda i, j: (i, 0)),
            pl.BlockSpec((1, d), lambda i, j: (0, 0)),
            pl.BlockSpec((tn, d), lambda i, j: (j, 0)),
            pl.BlockSpec((LANES, d), lambda i, j: (W_DT // LANES, 0)),
        ],
        out_specs=[
            pl.BlockSpec((tm, tn), lambda i, j: (i, j)),
            pl.BlockSpec((tm, LANES), lambda i, j: (i, 0)),
        ],
        out_shape=[
            jax.ShapeDtypeStruct((m, P_COLS), BF16),
            jax.ShapeDtypeStruct((m, LANES), F32),
        ],
        scratch_shapes=[pltpu.VMEM((tm, d), BF16)],
        compiler_params=_params(("parallel", "arbitrary")),
        name="in_proj",
    )(x, gain, wpt, wit)


def _ssd_kernel(z_ref, x_ref, b_ref, c_ref, dt_ref,
                cwx_ref, cwb_ref, cwc_ref, cbx_ref, cbb_ref, cbc_ref,
                dtb_ref, alog_ref, dsk_ref, nrm_ref, e_ref, selt_ref,
                o_ref, ext_ref, state_ref):
    c = pl.program_id(2)
    L = SSM_CHUNK
    GW = SSM_GROUP_WIDTH
    N = SSM_D_STATE
    XBC = GW + 2 * N

    @pl.when(c == 0)
    def _():
        ext_ref[0:CONV_HALO, :] = jnp.zeros((CONV_HALO, XBC), F32)
        state_ref[...] = jnp.zeros_like(state_ref)

    ext_ref[CONV_HALO:CONV_HALO + L, 0:GW] = x_ref[...].astype(F32)
    ext_ref[CONV_HALO:CONV_HALO + L, GW:GW + N] = b_ref[...].astype(F32)
    ext_ref[CONV_HALO:CONV_HALO + L, GW + N:XBC] = c_ref[...].astype(F32)

    cw = jnp.concatenate([cwx_ref[...], cwb_ref[...], cwc_ref[...]], axis=1)
    cbias = jnp.concatenate([cbx_ref[...], cbb_ref[...], cbc_ref[...]], axis=1)
    ext = ext_ref[...]
    acc = cw[0:1, :] * ext
    for k in range(1, SSM_CONV):
        acc = pltpu.roll(acc, 1, axis=0) + cw[k:k + 1, :] * ext
    ext_ref[0:CONV_HALO, :] = ext_ref[L:L + CONV_HALO, :]
    xbc = _silu(acc[CONV_HALO:, :] + cbias)
    xs = xbc[:, 0:GW]
    bm = xbc[:, GW:GW + N]
    cm = xbc[:, GW + N:XBC]

    dtr = dt_ref[...] + dtb_ref[...]
    dt = jnp.maximum(dtr, 0.0) + jnp.log1p(jnp.exp(-jnp.abs(dtr)))
    e01 = e_ref[...]
    dt_x = _sel_right(dt, e01)
    adt_c = dt * (-LOG2E * jnp.exp(alog_ref[...]))
    row = lax.broadcasted_iota(jnp.int32, (L, L), 0)
    col = lax.broadcasted_iota(jnp.int32, (L, L), 1)
    tril = row >= col
    tril01 = tril.astype(BF16)
    acum_c = _sel_left(tril01, adt_c)
    selt = selt_ref[...]
    hi, mid, lo = _split3(acum_c)
    acum_r = (_dot_nt(selt, lo) + _dot_nt(selt, mid)) + _dot_nt(selt, hi)
    acum_x = (_dot(lo, e01) + _dot(mid, e01)) + _dot(hi, e01)

    bt = bm.T.astype(BF16)
    cmb = cm.astype(BF16)
    H = L // 2
    cb_top = _dot(cmb[0:H, :], bt[:, 0:H])
    cb_bot = _dot(cmb[H:L, :], bt)
    xdt = xs * dt_x
    xdt_b = xdt.astype(BF16)

    tri = tril[0:H, 0:H]
    lane = lax.broadcasted_iota(jnp.int32, (L, LANES), 1)
    low_half = lane < SSM_HEAD_DIM
    pairs = []
    for pair in range(SSM_HEADS_PER_GROUP // 2):
        xp = xdt_b[:, pair * LANES:(pair + 1) * LANES]
        y_top = None
        y_bot = None
        for half in range(2):
            h = 2 * pair + half
            a_col = acum_x[:, h * SSM_HEAD_DIM:h * SSM_HEAD_DIM + 1]
            a_row = acum_r[h:h + 1, :]
            w00 = cb_top * jnp.exp2(jnp.where(tri, a_col[0:H] - a_row[:, 0:H], NEG))
            w10 = cb_bot[:, 0:H] * jnp.exp2(a_col[H:L] - a_row[:, 0:H])
            w11 = cb_bot[:, H:L] * jnp.exp2(jnp.where(tri, a_col[H:L] - a_row[:, H:L], NEG))
            keep = low_half if half == 0 else jnp.logical_not(low_half)
            xh = jnp.where(keep, xp, jnp.zeros_like(xp))
            yt = _dot(w00.astype(BF16), xh[0:H, :])
            yb = _dot(jnp.concatenate([w10, w11], axis=1).astype(BF16), xh)
            y_top = yt if y_top is None else y_top + yt
            y_bot = yb if y_bot is None else y_bot + yb
        pairs.append(jnp.concatenate([y_top, y_bot], axis=0))
    y_diag = jnp.concatenate(pairs, axis=1)

    state = state_ref[...]
    y_off = jnp.exp2(acum_x) * _dot(cmb, state.astype(BF16))
    y = y_diag + y_off + dsk_ref[...] * xs

    last = acum_x[L - 1:L, :]
    xdec = (xdt * jnp.exp2(last - acum_x)).astype(BF16)
    state_ref[...] = state * jnp.exp2(last) + _dot(bt, xdec)

    yg = y * _silu(z_ref[...].astype(F32))
    o_ref[...] = _rms(yg, nrm_ref[...]).astype(BF16)


def _ssd(p, dt_raw, conv_w, conv_b, dtb_row, alog_row, dskip_x, nrm_row, e01, selt, bsz, t_len):
    L = SSM_CHUNK
    nc = t_len // L
    GW = SSM_GROUP_WIDTH
    N = SSM_D_STATE
    rowblk = lambda b, g, c: b * nc + c
    return pl.pallas_call(
        _ssd_kernel,
        grid=(bsz, SSM_N_GROUPS, nc),
        in_specs=[
            pl.BlockSpec((L, GW), lambda b, g, c: (rowblk(b, g, c), PA_Z // GW + g)),
            pl.BlockSpec((L, GW), lambda b, g, c: (rowblk(b, g, c), PA_X // GW + g)),
            pl.BlockSpec((L, N), lambda b, g, c: (rowblk(b, g, c), PA_B // N + g)),
            pl.BlockSpec((L, N), lambda b, g, c: (rowblk(b, g, c), PA_C // N + g)),
            pl.BlockSpec((L, LANES), lambda b, g, c: (rowblk(b, g, c), 0)),
            pl.BlockSpec((SSM_CONV, GW), lambda b, g, c: (0, g)),
            pl.BlockSpec((SSM_CONV, N), lambda b, g, c: (0, SSM_D_INNER // N + g)),
            pl.BlockSpec((SSM_CONV, N), lambda b, g, c: (0, (SSM_D_INNER + SSM_GN) // N + g)),
            pl.BlockSpec((1, GW), lambda b, g, c: (0, g)),
            pl.BlockSpec((1, N), lambda b, g, c: (0, SSM_D_INNER // N + g)),
            pl.BlockSpec((1, N), lambda b, g, c: (0, (SSM_D_INNER + SSM_GN) // N + g)),
            pl.BlockSpec((1, LANES), lambda b, g, c: (0, 0)),
            pl.BlockSpec((1, LANES), lambda b, g, c: (0, 0)),
            pl.BlockSpec((1, GW), lambda b, g, c: (0, g)),
            pl.BlockSpec((1, GW), lambda b, g, c: (0, g)),
            pl.BlockSpec((None, LANES, GW), lambda b, g, c: (g, 0, 0)),
            pl.BlockSpec((None, SSM_HEADS_PER_GROUP, LANES), lambda b, g, c: (g, 0, 0)),
        ],
        out_specs=pl.BlockSpec((L, GW), lambda b, g, c: (rowblk(b, g, c), g)),
        out_shape=jax.ShapeDtypeStruct((bsz * t_len, SSM_D_INNER), BF16),
        scratch_shapes=[
            pltpu.VMEM((CONV_HALO + L, GW + 2 * N), F32),
            pltpu.VMEM((N, GW), F32),
        ],
        compiler_params=_params(("parallel", "parallel", "arbitrary")),
        name="ssd",
    )(p, p, p, p, dt_raw, conv_w, conv_w, conv_w, conv_b, conv_b, conv_b,
      dtb_row, alog_row, dskip_x, nrm_row, e01, selt)


def _half_rms_scale(x, low):
    sq = x * x
    zero = jnp.zeros_like(sq)
    ss_lo = jnp.sum(jnp.where(low, sq, zero), axis=-1, keepdims=True)
    ss_hi = jnp.sum(jnp.where(low, zero, sq), axis=-1, keepdims=True)
    inv = 1.0 / ATTN_HEAD_DIM
    return jnp.where(low, lax.rsqrt(ss_lo * inv + EPS), lax.rsqrt(ss_hi * inv + EPS))


RIDER_ROWS = 16


def _rider_span(rows, n_steps, max_span):
    span = 1
    while (rows * span) % (n_steps * RIDER_ROWS) != 0:
        span *= 2
        assert span <= max_span, (rows, n_steps)
    return span


def _attn_kernel(sink_ref, slope_ref, q_ref, kc_ref, kp_ref, vc_ref, vp_ref, qn_ref, kn_ref, *refs, rider_spans):
    nr = len(rider_spans)
    r_refs, o_ref, q_refs = refs[:nr], refs[nr], refs[nr + 1:]
    n = pl.program_id(1)
    blk = ATTN_BLOCK
    hd = ATTN_HEAD_DIM
    kf = jnp.concatenate([kp_ref[...], kc_ref[...]], axis=0).astype(F32)
    vf = jnp.concatenate([vp_ref[...], vc_ref[...]], axis=0).astype(F32)
    sj = lax.broadcasted_iota(jnp.int32, (2 * blk, blk), 0)
    qi = lax.broadcasted_iota(jnp.int32, (2 * blk, blk), 1)
    dist = qi + blk - sj
    valid = (dist >= 0) & (dist < WINDOW) & ((sj >= blk) | (n > 0))
    ndm = jnp.where(valid, -dist.astype(F32), NEG)
    npair = ATTN_Q_PER_KV // 2
    ndm4 = jnp.concatenate([ndm] * npair, axis=1)
    low_k = lax.broadcasted_iota(jnp.int32, (2 * blk, LANES), 1) < hd
    low_q = lax.broadcasted_iota(jnp.int32, (npair * blk, LANES), 1) < hd
    qn = qn_ref[...] * (LOG2E * ATTN_HEAD_DIM ** -0.5)
    kn = kn_ref[...]
    zk = jnp.zeros((2 * blk, LANES), F32)
    for kvp in range(ATTN_N_KV // 2):
        k2 = kf[:, kvp * LANES:(kvp + 1) * LANES]
        k2 = k2 * _half_rms_scale(k2, low_k) * kn
        v2 = vf[:, kvp * LANES:(kvp + 1) * LANES]
        k2r = pltpu.roll(k2, hd, axis=1)
        v2r = pltpu.roll(v2, hd, axis=1)
        for half in range(2):
            kv = 2 * kvp + half
            k_lo_src, k_hi_src = (k2, k2r) if half == 0 else (k2r, k2)
            v_lo_src, v_hi_src = (v2, v2r) if half == 0 else (v2r, v2)
            k_lo = jnp.where(low_k, k_lo_src, zk).astype(BF16)
            k_hi = jnp.where(low_k, zk, k_hi_src).astype(BF16)
            v_lo = jnp.where(low_k, v_lo_src, zk).astype(BF16)
            v_hi = jnp.where(low_k, zk, v_hi_src).astype(BF16)
            q4 = jnp.concatenate(
                [q_ref[:, (kv * npair + j) * LANES:(kv * npair + j + 1) * LANES] for j in range(npair)],
                axis=0).astype(F32)
            qb = (q4 * _half_rms_scale(q4, low_q) * qn).astype(BF16)
            o4 = None
            for e in range(2):
                u = 2 * kv + e
                st = _dot_nt(k_lo if e == 0 else k_hi, qb) + slope_ref[u:u + 1, :] * ndm4
                sink = sink_ref[u:u + 1, :] * LOG2E
                m = jnp.maximum(jnp.max(st, axis=0, keepdims=True), sink)
                p = jnp.exp2(st - m)
                denom = jnp.sum(p, axis=0, keepdims=True) + jnp.exp2(sink - m)
                pn = (p * (1.0 / denom)).astype(BF16)
                oe = lax.dot_general(pn, v_lo if e == 0 else v_hi, (((0,), (0,)), ((), ())),
                                     preferred_element_type=F32)
                o4 = oe if o4 is None else o4 + oe
            for j in range(npair):
                hp = kv * npair + j
                o_ref[:, hp * LANES:(hp + 1) * LANES] = o4[j * blk:(j + 1) * blk, :].astype(BF16)

    for r_ref, q_ref, span in zip(r_refs, q_refs, rider_spans):
        if span == 1:
            q_ref[...] = r_ref[...].astype(BF16)
        else:
            @pl.when(n % span == 0)
            def _(r_ref=r_ref, q_ref=q_ref):
                q_ref[...] = r_ref[...].astype(BF16)


def _head_table(per_head):
    npair = ATTN_Q_PER_KV // 2
    t = per_head.astype(F32).reshape(ATTN_N_KV, npair, 2).transpose(0, 2, 1)
    return jnp.repeat(t.reshape(2 * ATTN_N_KV, npair), LANES, axis=1)


def _attn(p, sinks, qn_row, kn_row, riders, bsz, t_len):
    blk = ATTN_BLOCK
    slopes = jnp.asarray(
        [2.0 ** (-8.0 * (h + 1) / ATTN_N_HEADS) for h in range(ATTN_N_HEADS)], F32)
    nb = t_len // blk
    cur = lambda b, n: b * nb + n
    prev = lambda b, n: b * nb + jnp.maximum(n - 1, 0)
    n_steps = bsz * nb
    spans = tuple(_rider_span(r.shape[0], n_steps, nb) for r in riders)
    rider_specs = [
        pl.BlockSpec((r.shape[0] * span // n_steps, r.shape[1]), lambda b, n, span=span: (cur(b, n) // span, 0))
        for r, span in zip(riders, spans)]
    outs = pl.pallas_call(
        functools.partial(_attn_kernel, rider_spans=spans),
        grid=(bsz, nb),
        in_specs=[
            pl.BlockSpec((2 * ATTN_N_KV, 4 * LANES), lambda b, n: (0, 0)),
            pl.BlockSpec((2 * ATTN_N_KV, 4 * LANES), lambda b, n: (0, 0)),
            pl.BlockSpec((blk, ATTN_D), lambda b, n: (cur(b, n), PB_Q // ATTN_D)),
            pl.BlockSpec((blk, ATTN_KV_D), lambda b, n: (cur(b, n), PB_K // ATTN_KV_D)),
            pl.BlockSpec((blk, ATTN_KV_D), lambda b, n: (prev(b, n), PB_K // ATTN_KV_D)),
            pl.BlockSpec((blk, ATTN_KV_D), lambda b, n: (cur(b, n), PB_V // ATTN_KV_D)),
            pl.BlockSpec((blk, ATTN_KV_D), lambda b, n: (prev(b, n), PB_V // ATTN_KV_D)),
            pl.BlockSpec((1, LANES), lambda b, n: (0, 0)),
            pl.BlockSpec((1, LANES), lambda b, n: (0, 0)),
        ] + rider_specs,
        out_specs=[pl.BlockSpec((blk, ATTN_D), lambda b, n: (cur(b, n), 0))] + rider_specs,
        out_shape=[jax.ShapeDtypeStruct((bsz * t_len, ATTN_D), BF16)]
        + [jax.ShapeDtypeStruct(r.shape, BF16) for r in riders],
        compiler_params=_params(("parallel", "arbitrary")),
        name="swa",
    )(_head_table(sinks), _head_table(slopes * LOG2E), p, p, p, p, p, qn_row, kn_row, *riders)
    return outs[0], outs[1:]


def _merge_kernel(ys_ref, ya_ref, gs_ref, ga_ref, wos_ref, woa_ref, o_ref):
    ms = _dot(ys_ref[...], wos_ref[...])
    ma = _dot(ya_ref[...], woa_ref[...])
    mg = (jax.nn.sigmoid(gs_ref[...].astype(F32)) * ms
          + jax.nn.sigmoid(ga_ref[...].astype(F32)) * ma)
    o_ref[...] = mg.astype(BF16)


def _merge(ys, ya, p, wos, woa, tm, tn):
    m = ys.shape[0]
    d = wos.shape[1]
    return pl.pallas_call(
        _merge_kernel,
        grid=(m // tm, d // tn),
        in_specs=[
            pl.BlockSpec((tm, SSM_D_INNER), lambda i, j: (i, 0)),
            pl.BlockSpec((tm, ATTN_D), lambda i, j: (i, 0)),
            pl.BlockSpec((tm, tn), lambda i, j: (i, PB_GS // tn + j)),
            pl.BlockSpec((tm, tn), lambda i, j: (i, PB_GA // tn + j)),
            pl.BlockSpec((SSM_D_INNER, tn), lambda i, j: (0, j)),
            pl.BlockSpec((ATTN_D, tn), lambda i, j: (0, j)),
        ],
        out_specs=pl.BlockSpec((tm, tn), lambda i, j: (i, j)),
        out_shape=jax.ShapeDtypeStruct((m, d), BF16),
        compiler_params=_params(("parallel", "arbitrary")),
        name="merge",
    )(ys, ya, p, p, wos, woa)


def _outproj_kernel(x_ref, m_ref, w_ref, o_ref, wb_ref):
    @pl.when(pl.program_id(0) == 0)
    def _():
        wb_ref[...] = w_ref[...].astype(BF16)

    o_ref[...] = x_ref[...] + _dot(m_ref[...], wb_ref[...])


def _outproj(x, mg, wout, tm):
    m, d = x.shape
    return pl.pallas_call(
        _outproj_kernel,
        grid=(m // tm,),
        in_specs=[
            pl.BlockSpec((tm, d), lambda i: (i, 0)),
            pl.BlockSpec((tm, d), lambda i: (i, 0)),
            pl.BlockSpec((d, d), lambda i: (0, 0), pipeline_mode=pl.Buffered(1)),
        ],
        out_specs=pl.BlockSpec((tm, d), lambda i: (i, 0)),
        out_shape=jax.ShapeDtypeStruct((m, d), F32),
        scratch_shapes=[pltpu.VMEM((d, d), BF16)],
        compiler_params=_params(("arbitrary",)),
        name="out_proj",
    )(x, mg, wout)


def _selection_constants():
    e = np.zeros((SSM_N_GROUPS, LANES, SSM_GROUP_WIDTH), np.float32)
    st = np.zeros((SSM_N_GROUPS, SSM_HEADS_PER_GROUP, LANES), np.float32)
    for g in range(SSM_N_GROUPS):
        for h in range(SSM_HEADS_PER_GROUP):
            e[g, g * SSM_HEADS_PER_GROUP + h, h * SSM_HEAD_DIM:(h + 1) * SSM_HEAD_DIM] = 1.0
            st[g, h, g * SSM_HEADS_PER_GROUP + h] = 1.0
    return jnp.asarray(e, BF16), jnp.asarray(st, BF16)


def _pad_lanes(v):
    return jnp.pad(v.astype(F32), (0, LANES - v.shape[0])).reshape(1, LANES)


def kernel(x, ffn1_norm, ffn1_w_gate, ffn1_w_up, ffn1_w_down, mix_norm, w_in, conv_w, conv_b, dt_bias, a_log, d_skip, ssm_norm, q_norm, k_norm, sinks, w_o_ssm, w_o_attn, w_out, ffn2_norm, ffn2_w_gate, ffn2_w_up, ffn2_w_down):
    bsz, t_len, d = x.shape
    m = bsz * t_len
    depth = ffn1_norm.shape[0]
    e01, selt = _selection_constants()
    xf = x.reshape(m, d)
    for l in range(depth):
        wit = jnp.swapaxes(w_in[l], 0, 1)

        xf, wpt = _ffn(xf, ffn1_norm[l].reshape(1, d), ffn1_w_gate[l].astype(BF16),
                       ffn1_w_up[l].astype(BF16), ffn1_w_down[l].astype(BF16), tm=TM, tf=FFN_TF, wit=wit)

        p, dt_raw = _inproj(xf, mix_norm[l].reshape(1, d), wpt, wit, tm=TM, tn=INPROJ_TN)

        y_ssm = _ssd(
            p, dt_raw, conv_w[l], conv_b[l].reshape(1, -1),
            _pad_lanes(dt_bias[l]), _pad_lanes(a_log[l]),
            jnp.repeat(d_skip[l].astype(F32), SSM_HEAD_DIM).reshape(1, SSM_D_INNER),
            ssm_norm[l].reshape(1, SSM_D_INNER), e01, selt, bsz, t_len)

        y_attn, (wos, woa, wg2, wu2, wd2) = _attn(
            p, sinks[l].astype(F32),
            jnp.tile(q_norm[l].astype(F32), LANES // ATTN_HEAD_DIM).reshape(1, LANES),
            jnp.tile(k_norm[l].astype(F32), LANES // ATTN_HEAD_DIM).reshape(1, LANES),
            (w_o_ssm[l], w_o_attn[l], ffn2_w_gate[l], ffn2_w_up[l], ffn2_w_down[l]), bsz, t_len)

        mg = _merge(y_ssm, y_attn, p, wos, woa, tm=TM, tn=MERGE_TN)
        xf = _outproj(xf, mg, w_out[l], tm=OUTPROJ_TM)

        xf = _ffn(xf, ffn2_norm[l].reshape(1, d), wg2, wu2, wd2, tm=TM, tf=FFN_TF)
    return xf.reshape(bsz, t_len, d)
```

```python
import functools

import jax
import jax.numpy as jnp
import numpy as np
from jax import lax
from jax.experimental import pallas as pl
from jax.experimental.pallas import tpu as pltpu

F32 = jnp.float32
BF16 = jnp.bfloat16

D_MODEL = 2048
SSM_D_INNER = 4096
SSM_HEAD_DIM = 64
SSM_N_HEADS = 64
SSM_N_GROUPS = 8
SSM_HEADS_PER_GROUP = SSM_N_HEADS // SSM_N_GROUPS
SSM_GROUP_WIDTH = SSM_D_INNER // SSM_N_GROUPS
SSM_D_STATE = 128
SSM_CONV = 4
SSM_CHUNK = 256
SSM_GN = SSM_N_GROUPS * SSM_D_STATE
ATTN_HEAD_DIM = 64
ATTN_N_HEADS = 32
ATTN_N_KV = 4
ATTN_Q_PER_KV = 8
ATTN_D = ATTN_N_HEADS * ATTN_HEAD_DIM
ATTN_KV_D = ATTN_N_KV * ATTN_HEAD_DIM
WINDOW = 128
ATTN_BLOCK = 128
D_FF = 5632
EPS = 1e-6
NEG = -1e30
LOG2E = 1.4426950408889634

LANES = 128
CONV_HALO = 8

PA_Z = 0
PA_X = PA_Z + SSM_D_INNER
PA_B = PA_X + SSM_D_INNER
PA_C = PA_B + SSM_GN
PA_COLS = PA_C + SSM_GN
PB_Q = PA_COLS
PB_K = PB_Q + ATTN_D
PB_V = PB_K + ATTN_KV_D
PB_GS = PB_V + ATTN_KV_D
PB_GA = PB_GS + D_MODEL
P_COLS = PB_GA + D_MODEL

W_DT = PA_COLS
W_Q = W_DT + SSM_N_HEADS

VMEM_LIMIT = 56 * 1024 * 1024

TM = 1024
FFN_TF = 512
INPROJ_TN = 1536
MERGE_TN = 512
OUTPROJ_TM = 512


def _params(sem):
    return pltpu.CompilerParams(dimension_semantics=sem, vmem_limit_bytes=VMEM_LIMIT)


def _rms(x, gain):
    return x * lax.rsqrt(jnp.mean(x * x, axis=-1, keepdims=True) + EPS) * gain


def _silu(x):
    return x * jax.nn.sigmoid(x)


def _dot(a, b):
    return jnp.dot(a, b, preferred_element_type=F32)


def _dot_nt(a, b):
    return lax.dot_general(a, b, (((1,), (1,)), ((), ())), preferred_element_type=F32)


def _split3(x):
    hi = x.astype(BF16)
    r1 = x - hi.astype(F32)
    mid = r1.astype(BF16)
    lo = (r1 - mid.astype(F32)).astype(BF16)
    return hi, mid, lo


def _sel_left(m01, x):
    hi, mid, lo = _split3(x)
    return (_dot(m01, lo) + _dot(m01, mid)) + _dot(m01, hi)


def _sel_right(x, m01):
    hi, mid, lo = _split3(x)
    return (_dot(lo, m01) + _dot(mid, m01)) + _dot(hi, m01)


PACK_ROWS = 256


def _ffn_kernel(x_ref, gain_ref, wg_ref, wu_ref, wd_ref, *refs, n_pack):
    if n_pack:
        wi_ref, o_ref, wp_ref, h_ref = refs
    else:
        o_ref, h_ref = refs
    j = pl.program_id(1)

    @pl.when(j == 0)
    def _():
        x = x_ref[...]
        h_ref[...] = _rms(x, gain_ref[...]).astype(BF16)
        o_ref[...] = x

    h = h_ref[...]
    g = _dot(h, wg_ref[...])
    u = _dot(h, wu_ref[...])
    a = (0.5 * _silu(g) * u).astype(BF16)
    o_ref[...] += _dot(a, wd_ref[...])

    if n_pack:
        wp_ref[...] = wi_ref[...].astype(BF16)


def _ffn(x, gain, wg, wu, wd, tm, tf, wit=None):
    m, d = x.shape
    dff = wg.shape[1]
    nj = dff // tf
    in_specs = [
        pl.BlockSpec((tm, d), lambda i, j: (i, 0)),
        pl.BlockSpec((1, d), lambda i, j: (0, 0)),
        pl.BlockSpec((d, tf), lambda i, j: (0, j)),
        pl.BlockSpec((d, tf), lambda i, j: (0, j)),
        pl.BlockSpec((tf, d), lambda i, j: (j, 0)),
    ]
    out_specs = [pl.BlockSpec((tm, d), lambda i, j: (i, 0))]
    out_shape = [jax.ShapeDtypeStruct((m, d), F32)]
    operands = [x, gain, wg, wu, wd]
    n_pack = 0
    if wit is not None:
        n_pack = P_COLS // PACK_ROWS
        assert n_pack <= (m // tm) * nj and PA_COLS % PACK_ROWS == 0
        blk = lambda i, j: jnp.minimum(i * nj + j, n_pack - 1)
        in_specs[0] = pl.BlockSpec((tm, d), lambda i, j: (i, 0), pipeline_mode=pl.Buffered(1))
        in_specs.append(pl.BlockSpec(
            (pl.Element(PACK_ROWS), pl.Element(d)),
            lambda i, j: (pl.multiple_of(
                blk(i, j) * PACK_ROWS + jnp.where(blk(i, j) * PACK_ROWS >= PA_COLS, SSM_N_HEADS, 0),
                SSM_N_HEADS), 0)))
        out_specs.append(pl.BlockSpec((PACK_ROWS, d), lambda i, j: (blk(i, j), 0)))
        out_shape.append(jax.ShapeDtypeStruct((P_COLS, d), BF16))
        operands.append(wit)
    outs = pl.pallas_call(
        functools.partial(_ffn_kernel, n_pack=n_pack),
        grid=(m // tm, nj),
        in_specs=in_specs,
        out_specs=out_specs,
        out_shape=out_shape,
        scratch_shapes=[pltpu.VMEM((tm, d), BF16)],
        compiler_params=_params(("arbitrary" if n_pack else "parallel", "arbitrary")),
        name="ffn",
    )(*operands)
    return outs if n_pack else outs[0]


def _inproj_kernel(x_ref, gain_ref, w_ref, wdt_ref, p_ref, dt_ref, h_ref):
    j = pl.program_id(1)

    @pl.when(j == 0)
    def _():
        h = _rms(x_ref[...], gain_ref[...]).astype(BF16)
        h_ref[...] = h
        dt = _dot_nt(h, wdt_ref[...].astype(BF16))
        lane = lax.broadcasted_iota(jnp.int32, dt.shape, 1)
        dt_ref[...] = jnp.where(lane < SSM_N_HEADS, dt, 0.0)

    p_ref[...] = _dot_nt(h_ref[...], w_ref[...]).astype(BF16)


def _inproj(x, gain, wpt, wit, tm, tn):
    m, d = x.shape
    return pl.pallas_call(
        _inproj_kernel,
        grid=(m // tm, P_COLS // tn),
        in_specs=[
            pl.BlockSpec((tm, d), lambda i, j: (i, 0)),
            pl.BlockSpec((1, d), lambda i, j: (0, 0)),
            pl.BlockSpec((tn, d), lambda i, j: (j, 0)),
            pl.BlockSpec((LANES, d), lambda i, j: (W_DT // LANES, 0)),
        ],
        out_specs=[
            pl.BlockSpec((tm, tn), lambda i, j: (i, j)),
            pl.BlockSpec((tm, LANES), lambda i, j: (i, 0)),
        ],
        out_shape=[
            jax.ShapeDtypeStruct((m, P_COLS), BF16),
            jax.ShapeDtypeStruct((m, LANES), F32),
        ],
        scratch_shapes=[pltpu.VMEM((tm, d), BF16)],
        compiler_params=_params(("parallel", "arbitrary")),
        name="in_proj",
    )(x, gain, wpt, wit)


def _ssd_kernel(z_ref, x_ref, b_ref, c_ref, dt_ref,
                cwx_ref, cwb_ref, cwc_ref, cbx_ref, cbb_ref, cbc_ref,
                dtb_ref, alog_ref, dsk_ref, nrm_ref, e_ref, selt_ref,
                o_ref, ext_ref, state_ref):
    c = pl.program_id(2)
    L = SSM_CHUNK
    GW = SSM_GROUP_WIDTH
    N = SSM_D_STATE
    XBC = GW + 2 * N

    @pl.when(c == 0)
    def _():
        ext_ref[0:CONV_HALO, :] = jnp.zeros((CONV_HALO, XBC), F32)
        state_ref[...] = jnp.zeros_like(state_ref)

    ext_ref[CONV_HALO:CONV_HALO + L, 0:GW] = x_ref[...].astype(F32)
    ext_ref[CONV_HALO:CONV_HALO + L, GW:GW + N] = b_ref[...].astype(F32)
    ext_ref[CONV_HALO:CONV_HALO + L, GW + N:XBC] = c_ref[...].astype(F32)

    cw = jnp.concatenate([cwx_ref[...], cwb_ref[...], cwc_ref[...]], axis=1)
    cbias = jnp.concatenate([cbx_ref[...], cbb_ref[...], cbc_ref[...]], axis=1)
    ext = ext_ref[...]
    acc = cw[0:1, :] * ext
    for k in range(1, SSM_CONV):
        acc = pltpu.roll(acc, 1, axis=0) + cw[k:k + 1, :] * ext
    ext_ref[0:CONV_HALO, :] = ext_ref[L:L + CONV_HALO, :]
    xbc = _silu(acc[CONV_HALO:, :] + cbias)
    xs = xbc[:, 0:GW]
    bm = xbc[:, GW:GW + N]
    cm = xbc[:, GW + N:XBC]

    dtr = dt_ref[...] + dtb_ref[...]
    dt = jnp.maximum(dtr, 0.0) + jnp.log1p(jnp.exp(-jnp.abs(dtr)))
    e01 = e_ref[...]
    dt_x = _sel_right(dt, e01)
    adt_c = dt * (-LOG2E * jnp.exp(alog_ref[...]))
    row = lax.broadcasted_iota(jnp.int32, (L, L), 0)
    col = lax.broadcasted_iota(jnp.int32, (L, L), 1)
    tril = row >= col
    tril01 = tril.astype(BF16)
    acum_c = _sel_left(tril01, adt_c)
    selt = selt_ref[...]
    hi, mid, lo = _split3(acum_c)
    acum_r = (_dot_nt(selt, lo) + _dot_nt(selt, mid)) + _dot_nt(selt, hi)
    acum_x = (_dot(lo, e01) + _dot(mid, e01)) + _dot(hi, e01)

    bt = bm.T.astype(BF16)
    cmb = cm.astype(BF16)
    H = L // 2
    cb_top = _dot(cmb[0:H, :], bt[:, 0:H])
    cb_bot = _dot(cmb[H:L, :], bt)
    xdt = xs * dt_x
    xdt_b = xdt.astype(BF16)

    tri = tril[0:H, 0:H]
    lane = lax.broadcasted_iota(jnp.int32, (L, LANES), 1)
    low_half = lane < SSM_HEAD_DIM
    pairs = []
    for pair in range(SSM_HEADS_PER_GROUP // 2):
        xp = xdt_b[:, pair * LANES:(pair + 1) * LANES]
        y_top = None
        y_bot = None
        for half in range(2):
            h = 2 * pair + half
            a_col = acum_x[:, h * SSM_HEAD_DIM:h * SSM_HEAD_DIM + 1]
            a_row = acum_r[h:h + 1, :]
            w00 = cb_top * jnp.exp2(jnp.where(tri, a_col[0:H] - a_row[:, 0:H], NEG))
            w10 = cb_bot[:, 0:H] * jnp.exp2(a_col[H:L] - a_row[:, 0:H])
            w11 = cb_bot[:, H:L] * jnp.exp2(jnp.where(tri, a_col[H:L] - a_row[:, H:L], NEG))
            keep = low_half if half == 0 else jnp.logical_not(low_half)
            xh = jnp.where(keep, xp, jnp.zeros_like(xp))
            yt = _dot(w00.astype(BF16), xh[0:H, :])
            yb = _dot(jnp.concatenate([w10, w11], axis=1).astype(BF16), xh)
            y_top = yt if y_top is None else y_top + yt
            y_bot = yb if y_bot is None else y_bot + yb
        pairs.append(jnp.concatenate([y_top, y_bot], axis=0))
    y_diag = jnp.concatenate(pairs, axis=1)

    state = state_ref[...]
    y_off = jnp.exp2(acum_x) * _dot(cmb, state.astype(BF16))
    y = y_diag + y_off + dsk_ref[...] * xs

    last = acum_x[L - 1:L, :]
    xdec = (xdt * jnp.exp2(last - acum_x)).astype(BF16)
    state_ref[...] = state * jnp.exp2(last) + _dot(bt, xdec)

    yg = y * _silu(z_ref[...].astype(F32))
    o_ref[...] = _rms(yg, nrm_ref[...]).astype(BF16)


def _ssd(p, dt_raw, conv_w, conv_b, dtb_row, alog_row, dskip_x, nrm_row, e01, selt, bsz, t_len):
    L = SSM_CHUNK
    nc = t_len // L
    GW = SSM_GROUP_WIDTH
    N = SSM_D_STATE
    rowblk = lambda b, g, c: b * nc + c
    return pl.pallas_call(
        _ssd_kernel,
        grid=(bsz, SSM_N_GROUPS, nc),
        in_specs=[
            pl.BlockSpec((L, GW), lambda b, g, c: (rowblk(b, g, c), PA_Z // GW + g)),
            pl.BlockSpec((L, GW), lambda b, g, c: (rowblk(b, g, c), PA_X // GW + g)),
            pl.BlockSpec((L, N), lambda b, g, c: (rowblk(b, g, c), PA_B // N + g)),
            pl.BlockSpec((L, N), lambda b, g, c: (rowblk(b, g, c), PA_C // N + g)),
            pl.BlockSpec((L, LANES), lambda b, g, c: (rowblk(b, g, c), 0)),
            pl.BlockSpec((SSM_CONV, GW), lambda b, g, c: (0, g)),
            pl.BlockSpec((SSM_CONV, N), lambda b, g, c: (0, SSM_D_INNER // N + g)),
            pl.BlockSpec((SSM_CONV, N), lambda b, g, c: (0, (SSM_D_INNER + SSM_GN) // N + g)),
            pl.BlockSpec((1, GW), lambda b, g, c: (0, g)),
            pl.BlockSpec((1, N), lambda b, g, c: (0, SSM_D_INNER // N + g)),
            pl.BlockSpec((1, N), lambda b, g, c: (0, (SSM_D_INNER + SSM_GN) // N + g)),
            pl.BlockSpec((1, LANES), lambda b, g, c: (0, 0)),
            pl.BlockSpec((1, LANES), lambda b, g, c: (0, 0)),
            pl.BlockSpec((1, GW), lambda b, g, c: (0, g)),
            pl.BlockSpec((1, GW), lambda b, g, c: (0, g)),
            pl.BlockSpec((None, LANES, GW), lambda b, g, c: (g, 0, 0)),
            pl.BlockSpec((None, SSM_HEADS_PER_GROUP, LANES), lambda b, g, c: (g, 0, 0)),
        ],
        out_specs=pl.BlockSpec((L, GW), lambda b, g, c: (rowblk(b, g, c), g)),
        out_shape=jax.ShapeDtypeStruct((bsz * t_len, SSM_D_INNER), BF16),
        scratch_shapes=[
            pltpu.VMEM((CONV_HALO + L, GW + 2 * N), F32),
            pltpu.VMEM((N, GW), F32),
        ],
        compiler_params=_params(("parallel", "parallel", "arbitrary")),
        name="ssd",
    )(p, p, p, p, dt_raw, conv_w, conv_w, conv_w, conv_b, conv_b, conv_b,
      dtb_row, alog_row, dskip_x, nrm_row, e01, selt)


def _half_rms_scale(x, low):
    sq = x * x
    zero = jnp.zeros_like(sq)
    ss_lo = jnp.sum(jnp.where(low, sq, zero), axis=-1, keepdims=True)
    ss_hi = jnp.sum(jnp.where(low, zero, sq), axis=-1, keepdims=True)
    inv = 1.0 / ATTN_HEAD_DIM
    return jnp.where(low, lax.rsqrt(ss_lo * inv + EPS), lax.rsqrt(ss_hi * inv + EPS))


RIDER_ROWS = 16


def _rider_span(rows, n_steps, max_span):
    span = 1
    while (rows * span) % (n_steps * RIDER_ROWS) != 0:
        span *= 2
        assert span <= max_span, (rows, n_steps)
    return span


def _attn_kernel(sink_ref, slope_ref, q_ref, kc_ref, kp_ref, vc_ref, vp_ref, qn_ref, kn_ref, *refs, rider_spans):
    nr = len(rider_spans)
    r_refs, o_ref, q_refs = refs[:nr], refs[nr], refs[nr + 1:]
    n = pl.program_id(1)
    blk = ATTN_BLOCK
    hd = ATTN_HEAD_DIM
    kf = jnp.concatenate([kp_ref[...], kc_ref[...]], axis=0).astype(F32)
    vf = jnp.concatenate([vp_ref[...], vc_ref[...]], axis=0).astype(F32)
    sj = lax.broadcasted_iota(jnp.int32, (2 * blk, blk), 0)
    qi = lax.broadcasted_iota(jnp.int32, (2 * blk, blk), 1)
    dist = qi + blk - sj
    valid = (dist >= 0) & (dist < WINDOW) & ((sj >= blk) | (n > 0))
    ndm = jnp.where(valid, -dist.astype(F32), NEG)
    npair = ATTN_Q_PER_KV // 2
    ndm4 = jnp.concatenate([ndm] * npair, axis=1)
    low_k = lax.broadcasted_iota(jnp.int32, (2 * blk, LANES), 1) < hd
    low_q = lax.broadcasted_iota(jnp.int32, (npair * blk, LANES), 1) < hd
    qn = qn_ref[...] * (LOG2E * ATTN_HEAD_DIM ** -0.5)
    kn = kn_ref[...]
    zk = jnp.zeros((2 * blk, LANES), F32)
    for kvp in range(ATTN_N_KV // 2):
        k2 = kf[:, kvp * LANES:(kvp + 1) * LANES]
        k2 = k2 * _half_rms_scale(k2, low_k) * kn
        v2 = vf[:, kvp * LANES:(kvp + 1) * LANES]
        k2r = pltpu.roll(k2, hd, axis=1)
        v2r = pltpu.roll(v2, hd, axis=1)
        for half in range(2):
            kv = 2 * kvp + half
            k_lo_src, k_hi_src = (k2, k2r) if half == 0 else (k2r, k2)
            v_lo_src, v_hi_src = (v2, v2r) if half == 0 else (v2r, v2)
            k_lo = jnp.where(low_k, k_lo_src, zk).astype(BF16)
            k_hi = jnp.where(low_k, zk, k_hi_src).astype(BF16)
            v_lo = jnp.where(low_k, v_lo_src, zk).astype(BF16)
            v_hi = jnp.where(low_k, zk, v_hi_src).astype(BF16)
            q4 = jnp.concatenate(
                [q_ref[:, (kv * npair + j) * LANES:(kv * npair + j + 1) * LANES] for j in range(npair)],
                axis=0).astype(F32)
            qb = (q4 * _half_rms_scale(q4, low_q) * qn).astype(BF16)
            o4 = None
            for e in range(2):
                u = 2 * kv + e
                st = _dot_nt(k_lo if e == 0 else k_hi, qb) + slope_ref[u:u + 1, :] * ndm4
                sink = sink_ref[u:u + 1, :] * LOG2E
                m = jnp.maximum(jnp.max(st, axis=0, keepdims=True), sink)
                p = jnp.exp2(st - m)
                denom = jnp.sum(p, axis=0, keepdims=True) + jnp.exp2(sink - m)
                pn = (p * (1.0 / denom)).astype(BF16)
                oe = lax.dot_general(pn, v_lo if e == 0 else v_hi, (((0,), (0,)), ((), ())),
                                     preferred_element_type=F32)
                o4 = oe if o4 is None else o4 + oe
            for j in range(npair):
                hp = kv * npair + j
                o_ref[:, hp * LANES:(hp + 1) * LANES] = o4[j * blk:(j + 1) * blk, :].astype(BF16)

    for r_ref, q_ref, span in zip(r_refs, q_refs, rider_spans):
        if span == 1:
            q_ref[...] = r_ref[...].astype(BF16)
        else:
            @pl.when(n % span == 0)
            def _(r_ref=r_ref, q_ref=q_ref):
                q_ref[...] = r_ref[...].astype(BF16)


def _head_table(per_head):
    npair = ATTN_Q_PER_KV // 2
    t = per_head.astype(F32).reshape(ATTN_N_KV, npair, 2).transpose(0, 2, 1)
    return jnp.repeat(t.reshape(2 * ATTN_N_KV, npair), LANES, axis=1)


def _attn(p, sinks, qn_row, kn_row, riders, bsz, t_len):
    blk = ATTN_BLOCK
    slopes = jnp.asarray(
        [2.0 ** (-8.0 * (h + 1) / ATTN_N_HEADS) for h in range(ATTN_N_HEADS)], F32)
    nb = t_len // blk
    cur = lambda b, n: b * nb + n
    prev = lambda b, n: b * nb + jnp.maximum(n - 1, 0)
    n_steps = bsz * nb
    spans = tuple(_rider_span(r.shape[0], n_steps, nb) for r in riders)
    rider_specs = [
        pl.BlockSpec((r.shape[0] * span // n_steps, r.shape[1]), lambda b, n, span=span: (cur(b, n) // span, 0))
        for r, span in zip(riders, spans)]
    outs = pl.pallas_call(
        functools.partial(_attn_kernel, rider_spans=spans),
        grid=(bsz, nb),
        in_specs=[
            pl.BlockSpec((2 * ATTN_N_KV, 4 * LANES), lambda b, n: (0, 0)),
            pl.BlockSpec((2 * ATTN_N_KV, 4 * LANES), lambda b, n: (0, 0)),
            pl.BlockSpec((blk, ATTN_D), lambda b, n: (cur(b, n), PB_Q // ATTN_D)),
            pl.BlockSpec((blk, ATTN_KV_D), lambda b, n: (cur(b, n), PB_K // ATTN_KV_D)),
            pl.BlockSpec((blk, ATTN_KV_D), lambda b, n: (prev(b, n), PB_K // ATTN_KV_D)),
            pl.BlockSpec((blk, ATTN_KV_D), lambda b, n: (cur(b, n), PB_V // ATTN_KV_D)),
            pl.BlockSpec((blk, ATTN_KV_D), lambda b, n: (prev(b, n), PB_V // ATTN_KV_D)),
            pl.BlockSpec((1, LANES), lambda b, n: (0, 0)),
            pl.BlockSpec((1, LANES), lambda b, n: (0, 0)),
        ] + rider_specs,
        out_specs=[pl.BlockSpec((blk, ATTN_D), lambda b, n: (cur(b, n), 0))] + rider_specs,
        out_shape=[jax.ShapeDtypeStruct((bsz * t_len, ATTN_D), BF16)]
        + [jax.ShapeDtypeStruct(r.shape, BF16) for r in riders],
        compiler_params=_params(("parallel", "arbitrary")),
        name="swa",
    )(_head_table(sinks), _head_table(slopes * LOG2E), p, p, p, p, p, qn_row, kn_row, *riders)
    return outs[0], outs[1:]


def _merge_kernel(ys_ref, ya_ref, gs_ref, ga_ref, wos_ref, woa_ref, o_ref):
    ms = _dot(ys_ref[...], wos_ref[...])
    ma = _dot(ya_ref[...], woa_ref[...])
    mg = (jax.nn.sigmoid(gs_ref[...].astype(F32)) * ms
          + jax.nn.sigmoid(ga_ref[...].astype(F32)) * ma)
    o_ref[...] = mg.astype(BF16)


def _merge(ys, ya, p, wos, woa, tm, tn):
    m = ys.shape[0]
    d = wos.shape[1]
    return pl.pallas_call(
        _merge_kernel,
        grid=(m // tm, d // tn),
        in_specs=[
            pl.BlockSpec((tm, SSM_D_INNER), lambda i, j: (i, 0)),
            pl.BlockSpec((tm, ATTN_D), lambda i, j: (i, 0)),
            pl.BlockSpec((tm, tn), lambda i, j: (i, PB_GS // tn + j)),
            pl.BlockSpec((tm, tn), lambda i, j: (i, PB_GA // tn + j)),
            pl.BlockSpec((SSM_D_INNER, tn), lambda i, j: (0, j)),
            pl.BlockSpec((ATTN_D, tn), lambda i, j: (0, j)),
        ],
        out_specs=pl.BlockSpec((tm, tn), lambda i, j: (i, j)),
        out_shape=jax.ShapeDtypeStruct((m, d), BF16),
        compiler_params=_params(("parallel", "arbitrary")),
        name="merge",
    )(ys, ya, p, p, wos, woa)


def _outproj_kernel(x_ref, m_ref, w_ref, o_ref, wb_ref):
    @pl.when(pl.program_id(0) == 0)
    def _():
        wb_ref[...] = w_ref[...].astype(BF16)

    o_ref[...] = x_ref[...] + _dot(m_ref[...], wb_ref[...])


def _outproj(x, mg, wout, tm):
    m, d = x.shape
    return pl.pallas_call(
        _outproj_kernel,
        grid=(m // tm,),
        in_specs=[
            pl.BlockSpec((tm, d), lambda i: (i, 0)),
            pl.BlockSpec((tm, d), lambda i: (i, 0)),
            pl.BlockSpec((d, d), lambda i: (0, 0), pipeline_mode=pl.Buffered(1)),
        ],
        out_specs=pl.BlockSpec((tm, d), lambda i: (i, 0)),
        out_shape=jax.ShapeDtypeStruct((m, d), F32),
        scratch_shapes=[pltpu.VMEM((d, d), BF16)],
        compiler_params=_params(("arbitrary",)),
        name="out_proj",
    )(x, mg, wout)


def _selection_constants():
    e = np.zeros((SSM_N_GROUPS, LANES, SSM_GROUP_WIDTH), np.float32)
    st = np.zeros((SSM_N_GROUPS, SSM_HEADS_PER_GROUP, LANES), np.float32)
    for g in range(SSM_N_GROUPS):
        for h in range(SSM_HEADS_PER_GROUP):
            e[g, g * SSM_HEADS_PER_GROUP + h, h * SSM_HEAD_DIM:(h + 1) * SSM_HEAD_DIM] = 1.0
            st[g, h, g * SSM_HEADS_PER_GROUP + h] = 1.0
    return jnp.asarray(e, BF16), jnp.asarray(st, BF16)


def _pad_lanes(v):
    return jnp.pad(v.astype(F32), (0, LANES - v.shape[0])).reshape(1, LANES)


def kernel(x, ffn1_norm, ffn1_w_gate, ffn1_w_up, ffn1_w_down, mix_norm, w_in, conv_w, conv_b, dt_bias, a_log, d_skip, ssm_norm, q_norm, k_norm, sinks, w_o_ssm, w_o_attn, w_out, ffn2_norm, ffn2_w_gate, ffn2_w_up, ffn2_w_down):
    bsz, t_len, d = x.shape
    m = bsz * t_len
    depth = ffn1_norm.shape[0]
    e01, selt = _selection_constants()
    xf = x.reshape(m, d)
    for l in range(depth):
        wit = jnp.swapaxes(w_in[l], 0, 1)

        xf, wpt = _ffn(xf, ffn1_norm[l].reshape(1, d), ffn1_w_gate[l].astype(BF16),
                       ffn1_w_up[l].astype(BF16), ffn1_w_down[l].astype(BF16), tm=TM, tf=FFN_TF, wit=wit)

        p, dt_raw = _inproj(xf, mix_norm[l].reshape(1, d), wpt, wit, tm=TM, tn=INPROJ_TN)

        y_ssm = _ssd(
            p, dt_raw, conv_w[l], conv_b[l].reshape(1, -1),
            _pad_lanes(dt_bias[l]), _pad_lanes(a_log[l]),
            jnp.repeat(d_skip[l].astype(F32), SSM_HEAD_DIM).reshape(1, SSM_D_INNER),
            ssm_norm[l].reshape(1, SSM_D_INNER), e01, selt, bsz, t_len)

        y_attn, (wos, woa, wg2, wu2, wd2) = _attn(
            p, sinks[l].astype(F32),
            jnp.tile(q_norm[l].astype(F32), LANES // ATTN_HEAD_DIM).reshape(1, LANES),
            jnp.tile(k_norm[l].astype(F32), LANES // ATTN_HEAD_DIM).reshape(1, LANES),
            (w_o_ssm[l], w_o_attn[l], ffn2_w_gate[l], ffn2_w_up[l], ffn2_w_down[l]), bsz, t_len)

        mg = _merge(y_ssm, y_attn, p, wos, woa, tm=TM, tn=MERGE_TN)
        xf = _outproj(xf, mg, w_out[l], tm=OUTPROJ_TM)

        xf = _ffn(xf, ffn2_norm[l].reshape(1, d), wg2, wu2, wd2, tm=TM, tf=FFN_TF)
    return xf.reshape(bsz, t_len, d)
```

```python
import functools

import jax
import jax.numpy as jnp
import numpy as np
from jax import lax
from jax.experimental import pallas as pl
from jax.experimental.pallas import tpu as pltpu

F32 = jnp.float32
BF16 = jnp.bfloat16

D_MODEL = 2048
SSM_D_INNER = 4096
SSM_HEAD_DIM = 64
SSM_N_HEADS = 64
SSM_N_GROUPS = 8
SSM_HEADS_PER_GROUP = SSM_N_HEADS // SSM_N_GROUPS
SSM_GROUP_WIDTH = SSM_D_INNER // SSM_N_GROUPS
SSM_D_STATE = 128
SSM_CONV = 4
SSM_CHUNK = 256
SSM_GN = SSM_N_GROUPS * SSM_D_STATE
ATTN_HEAD_DIM = 64
ATTN_N_HEADS = 32
ATTN_N_KV = 4
ATTN_Q_PER_KV = 8
ATTN_D = ATTN_N_HEADS * ATTN_HEAD_DIM
ATTN_KV_D = ATTN_N_KV * ATTN_HEAD_DIM
WINDOW = 128
ATTN_BLOCK = 128
D_FF = 5632
EPS = 1e-6
NEG = -1e30
LOG2E = 1.4426950408889634

LANES = 128
CONV_HALO = 8

PA_Z = 0
PA_X = PA_Z + SSM_D_INNER
PA_B = PA_X + SSM_D_INNER
PA_C = PA_B + SSM_GN
PA_COLS = PA_C + SSM_GN
PB_Q = PA_COLS
PB_K = PB_Q + ATTN_D
PB_V = PB_K + ATTN_KV_D
PB_GS = PB_V + ATTN_KV_D
PB_GA = PB_GS + D_MODEL
P_COLS = PB_GA + D_MODEL

W_DT = PA_COLS
W_Q = W_DT + SSM_N_HEADS

VMEM_LIMIT = 56 * 1024 * 1024

TM = 1024
FFN_TF = 512
INPROJ_TN = 1536
MERGE_TN = 512
OUTPROJ_TM = 512


def _params(sem):
    return pltpu.CompilerParams(dimension_semantics=sem, vmem_limit_bytes=VMEM_LIMIT)


def _rms(x, gain):
    return x * lax.rsqrt(jnp.mean(x * x, axis=-1, keepdims=True) + EPS) * gain


def _silu(x):
    return x * jax.nn.sigmoid(x)


def _dot(a, b):
    return jnp.dot(a, b, preferred_element_type=F32)


def _dot_nt(a, b):
    return lax.dot_general(a, b, (((1,), (1,)), ((), ())), preferred_element_type=F32)


def _split3(x):
    hi = x.astype(BF16)
    r1 = x - hi.astype(F32)
    mid = r1.astype(BF16)
    lo = (r1 - mid.astype(F32)).astype(BF16)
    return hi, mid, lo


def _sel_left(m01, x):
    hi, mid, lo = _split3(x)
    return (_dot(m01, lo) + _dot(m01, mid)) + _dot(m01, hi)


def _sel_right(x, m01):
    hi, mid, lo = _split3(x)
    return (_dot(lo, m01) + _dot(mid, m01)) + _dot(hi, m01)


PACK_ROWS = 256


def _ffn_kernel(x_ref, gain_ref, wg_ref, wu_ref, wd_ref, *refs, n_pack):
    if n_pack:
        wi_ref, o_ref, wp_ref, h_ref = refs
    else:
        o_ref, h_ref = refs
    i = pl.program_id(0)
    j = pl.program_id(1)

    @pl.when(j == 0)
    def _():
        x = x_ref[...]
        h_ref[...] = _rms(x, gain_ref[...]).astype(BF16)
        o_ref[...] = x

    h = h_ref[...]
    g = _dot(h, wg_ref[...])
    u = _dot(h, wu_ref[...])
    a = (0.5 * _silu(g) * u).astype(BF16)
    o_ref[...] += _dot(a, wd_ref[...])

    if n_pack:
        @pl.when(i * pl.num_programs(1) + j < n_pack)
        def _():
            wp_ref[...] = wi_ref[...].astype(BF16)


def _ffn(x, gain, wg, wu, wd, tm, tf, wit=None):
    m, d = x.shape
    dff = wg.shape[1]
    nj = dff // tf
    in_specs = [
        pl.BlockSpec((tm, d), lambda i, j: (i, 0)),
        pl.BlockSpec((1, d), lambda i, j: (0, 0)),
        pl.BlockSpec((d, tf), lambda i, j: (0, j)),
        pl.BlockSpec((d, tf), lambda i, j: (0, j)),
        pl.BlockSpec((tf, d), lambda i, j: (j, 0)),
    ]
    out_specs = [pl.BlockSpec((tm, d), lambda i, j: (i, 0))]
    out_shape = [jax.ShapeDtypeStruct((m, d), F32)]
    operands = [x, gain, wg, wu, wd]
    n_pack = 0
    if wit is not None:
        n_pack = P_COLS // PACK_ROWS
        assert n_pack <= (m // tm) * nj and PA_COLS % PACK_ROWS == 0
        blk = lambda i, j: jnp.minimum(i * nj + j, n_pack - 1)
        in_specs[0] = pl.BlockSpec((tm, d), lambda i, j: (i, 0), pipeline_mode=pl.Buffered(1))
        in_specs.append(pl.BlockSpec(
            (pl.Element(PACK_ROWS), pl.Element(d)),
            lambda i, j: (pl.multiple_of(
                blk(i, j) * PACK_ROWS + jnp.where(blk(i, j) * PACK_ROWS >= PA_COLS, SSM_N_HEADS, 0),
                SSM_N_HEADS), 0)))
        out_specs.append(pl.BlockSpec((PACK_ROWS, d), lambda i, j: (blk(i, j), 0)))
        out_shape.append(jax.ShapeDtypeStruct((P_COLS, d), BF16))
        operands.append(wit)
    outs = pl.pallas_call(
        functools.partial(_ffn_kernel, n_pack=n_pack),
        grid=(m // tm, nj),
        in_specs=in_specs,
        out_specs=out_specs,
        out_shape=out_shape,
        scratch_shapes=[pltpu.VMEM((tm, d), BF16)],
        compiler_params=_params(("arbitrary" if n_pack else "parallel", "arbitrary")),
        name="ffn",
    )(*operands)
    return outs if n_pack else outs[0]


def _inproj_kernel(x_ref, gain_ref, w_ref, wdt_ref, p_ref, dt_ref, h_ref):
    j = pl.program_id(1)

    @pl.when(j == 0)
    def _():
        h = _rms(x_ref[...], gain_ref[...]).astype(BF16)
        h_ref[...] = h
        dt = _dot_nt(h, wdt_ref[...].astype(BF16))
        lane = lax.broadcasted_iota(jnp.int32, dt.shape, 1)
        dt_ref[...] = jnp.where(lane < SSM_N_HEADS, dt, 0.0)

    p_ref[...] = _dot_nt(h_ref[...], w_ref[...]).astype(BF16)


def _inproj(x, gain, wpt, wit, tm, tn):
    m, d = x.shape
    return pl.pallas_call(
        _inproj_kernel,
        grid=(m // tm, P_COLS // tn),
        in_specs=[
            pl.BlockSpec((tm, d), lambda i, j: (i, 0)),
            pl.BlockSpec((1, d), lambda i, j: (0, 0)),
            pl.BlockSpec((tn, d), lambda i, j: (j, 0)),
            pl.BlockSpec((LANES, d), lambda i, j: (W_DT // LANES, 0)),
        ],
        out_specs=[
            pl.BlockSpec((tm, tn), lambda i, j: (i, j)),
            pl.BlockSpec((tm, LANES), lambda i, j: (i, 0)),
        ],
        out_shape=[
            jax.ShapeDtypeStruct((m, P_COLS), BF16),
            jax.ShapeDtypeStruct((m, LANES), F32),
        ],
        scratch_shapes=[pltpu.VMEM((tm, d), BF16)],
        compiler_params=_params(("parallel", "arbitrary")),
        name="in_proj",
    )(x, gain, wpt, wit)


SSD_GROUPS_PER_STEP = 8


def _ssd_kernel(z_ref, x_ref, b_ref, c_ref, dt_ref,
                cwx_ref, cwb_ref, cwc_ref, cbx_ref, cbb_ref, cbc_ref,
                dtb_ref, alog_ref, dsk_ref, nrm_ref, e_ref, selt_ref,
                o_ref, ext_ref, state_ref):
    GW = SSM_GROUP_WIDTH
    N = SSM_D_STATE
    for gi in range(SSD_GROUPS_PER_STEP):
        def part(ref, width, gi=gi):
            return ref.at[:, gi * width:(gi + 1) * width]
        _ssd_group(part(z_ref, GW), part(x_ref, GW), part(b_ref, N), part(c_ref, N), dt_ref,
                   part(cwx_ref, GW), part(cwb_ref, N), part(cwc_ref, N),
                   part(cbx_ref, GW), part(cbb_ref, N), part(cbc_ref, N),
                   dtb_ref, alog_ref, part(dsk_ref, GW), part(nrm_ref, GW), e_ref.at[gi], selt_ref.at[gi],
                   part(o_ref, GW), ext_ref.at[gi], state_ref.at[gi])


def _ssd_group(z_ref, x_ref, b_ref, c_ref, dt_ref,
               cwx_ref, cwb_ref, cwc_ref, cbx_ref, cbb_ref, cbc_ref,
               dtb_ref, alog_ref, dsk_ref, nrm_ref, e_ref, selt_ref,
               o_ref, ext_ref, state_ref):
    c = pl.program_id(2)
    L = SSM_CHUNK
    GW = SSM_GROUP_WIDTH
    N = SSM_D_STATE
    XBC = GW + 2 * N

    @pl.when(c == 0)
    def _():
        ext_ref[0:CONV_HALO, :] = jnp.zeros((CONV_HALO, XBC), F32)
        state_ref[...] = jnp.zeros_like(state_ref)

    ext_ref[CONV_HALO:CONV_HALO + L, 0:GW] = x_ref[...].astype(F32)
    ext_ref[CONV_HALO:CONV_HALO + L, GW:GW + N] = b_ref[...].astype(F32)
    ext_ref[CONV_HALO:CONV_HALO + L, GW + N:XBC] = c_ref[...].astype(F32)

    cw = jnp.concatenate([cwx_ref[...], cwb_ref[...], cwc_ref[...]], axis=1)
    cbias = jnp.concatenate([cbx_ref[...], cbb_ref[...], cbc_ref[...]], axis=1)
    ext = ext_ref[...]
    acc = cw[0:1, :] * ext
    for k in range(1, SSM_CONV):
        acc = pltpu.roll(acc, 1, axis=0) + cw[k:k + 1, :] * ext
    ext_ref[0:CONV_HALO, :] = ext_ref[L:L + CONV_HALO, :]
    xbc = _silu(acc[CONV_HALO:, :] + cbias)
    xs = xbc[:, 0:GW]
    bm = xbc[:, GW:GW + N]
    cm = xbc[:, GW + N:XBC]

    dtr = dt_ref[...] + dtb_ref[...]
    dt = jnp.maximum(dtr, 0.0) + jnp.log1p(jnp.exp(-jnp.abs(dtr)))
    e01 = e_ref[...]
    dt_x = _sel_right(dt, e01)
    adt_c = dt * (-LOG2E * jnp.exp(alog_ref[...]))
    row = lax.broadcasted_iota(jnp.int32, (L, L), 0)
    col = lax.broadcasted_iota(jnp.int32, (L, L), 1)
    tril = row >= col
    tril01 = tril.astype(BF16)
    acum_c = _sel_left(tril01, adt_c)
    selt = selt_ref[...]
    hi, mid, lo = _split3(acum_c)
    acum_r = (_dot_nt(selt, lo) + _dot_nt(selt, mid)) + _dot_nt(selt, hi)
    acum_x = (_dot(lo, e01) + _dot(mid, e01)) + _dot(hi, e01)

    bt = bm.T.astype(BF16)
    cmb = cm.astype(BF16)
    H = L // 2
    cb_top = _dot(cmb[0:H, :], bt[:, 0:H])
    cb_bot = _dot(cmb[H:L, :], bt)
    xdt = xs * dt_x
    xdt_b = xdt.astype(BF16)

    tri = tril[0:H, 0:H]
    lane = lax.broadcasted_iota(jnp.int32, (L, LANES), 1)
    low_half = lane < SSM_HEAD_DIM
    pairs = []
    for pair in range(SSM_HEADS_PER_GROUP // 2):
        xp = xdt_b[:, pair * LANES:(pair + 1) * LANES]
        y_top = None
        y_bot = None
        for half in range(2):
            h = 2 * pair + half
            a_col = acum_x[:, h * SSM_HEAD_DIM:h * SSM_HEAD_DIM + 1]
            a_row = acum_r[h:h + 1, :]
            w00 = cb_top * jnp.exp2(jnp.where(tri, a_col[0:H] - a_row[:, 0:H], NEG))
            w10 = cb_bot[:, 0:H] * jnp.exp2(a_col[H:L] - a_row[:, 0:H])
            w11 = cb_bot[:, H:L] * jnp.exp2(jnp.where(tri, a_col[H:L] - a_row[:, H:L], NEG))
            keep = low_half if half == 0 else jnp.logical_not(low_half)
            xh = jnp.where(keep, xp, jnp.zeros_like(xp))
            yt = _dot(w00.astype(BF16), xh[0:H, :])
            yb = _dot(jnp.concatenate([w10, w11], axis=1).astype(BF16), xh)
            y_top = yt if y_top is None else y_top + yt
            y_bot = yb if y_bot is None else y_bot + yb
        pairs.append(jnp.concatenate([y_top, y_bot], axis=0))
    y_diag = jnp.concatenate(pairs, axis=1)

    state = state_ref[...]
    y_off = jnp.exp2(acum_x) * _dot(cmb, state.astype(BF16))
    y = y_diag + y_off + dsk_ref[...] * xs

    last = acum_x[L - 1:L, :]
    xdec = (xdt * jnp.exp2(last - acum_x)).astype(BF16)
    state_ref[...] = state * jnp.exp2(last) + _dot(bt, xdec)

    yg = y * _silu(z_ref[...].astype(F32))
    o_ref[...] = _rms(yg, nrm_ref[...]).astype(BF16)


def _ssd(p, dt_raw, conv_w, conv_b, dtb_row, alog_row, dskip_x, nrm_row, e01, selt, bsz, t_len):
    L = SSM_CHUNK
    nc = t_len // L
    G = SSD_GROUPS_PER_STEP
    GW = G * SSM_GROUP_WIDTH
    N = G * SSM_D_STATE
    rowblk = lambda b, g, c: b * nc + c
    return pl.pallas_call(
        _ssd_kernel,
        grid=(bsz, SSM_N_GROUPS // G, nc),
        in_specs=[
            pl.BlockSpec((L, GW), lambda b, g, c: (rowblk(b, g, c), PA_Z // GW + g)),
            pl.BlockSpec((L, GW), lambda b, g, c: (rowblk(b, g, c), PA_X // GW + g)),
            pl.BlockSpec((L, N), lambda b, g, c: (rowblk(b, g, c), PA_B // N + g)),
            pl.BlockSpec((L, N), lambda b, g, c: (rowblk(b, g, c), PA_C // N + g)),
            pl.BlockSpec((L, LANES), lambda b, g, c: (rowblk(b, g, c), 0)),
            pl.BlockSpec((SSM_CONV, GW), lambda b, g, c: (0, g)),
            pl.BlockSpec((SSM_CONV, N), lambda b, g, c: (0, SSM_D_INNER // N + g)),
            pl.BlockSpec((SSM_CONV, N), lambda b, g, c: (0, (SSM_D_INNER + SSM_GN) // N + g)),
            pl.BlockSpec((1, GW), lambda b, g, c: (0, g)),
            pl.BlockSpec((1, N), lambda b, g, c: (0, SSM_D_INNER // N + g)),
            pl.BlockSpec((1, N), lambda b, g, c: (0, (SSM_D_INNER + SSM_GN) // N + g)),
            pl.BlockSpec((1, LANES), lambda b, g, c: (0, 0)),
            pl.BlockSpec((1, LANES), lambda b, g, c: (0, 0)),
            pl.BlockSpec((1, GW), lambda b, g, c: (0, g)),
            pl.BlockSpec((1, GW), lambda b, g, c: (0, g)),
            pl.BlockSpec((G, LANES, SSM_GROUP_WIDTH), lambda b, g, c: (g, 0, 0)),
            pl.BlockSpec((G, SSM_HEADS_PER_GROUP, LANES), lambda b, g, c: (g, 0, 0)),
        ],
        out_specs=pl.BlockSpec((L, GW), lambda b, g, c: (rowblk(b, g, c), g)),
        out_shape=jax.ShapeDtypeStruct((bsz * t_len, SSM_D_INNER), BF16),
        scratch_shapes=[
            pltpu.VMEM((G, CONV_HALO + L, SSM_GROUP_WIDTH + 2 * SSM_D_STATE), F32),
            pltpu.VMEM((G, SSM_D_STATE, SSM_GROUP_WIDTH), F32),
        ],
        compiler_params=_params(("parallel", "parallel", "arbitrary")),
        name="ssd",
    )(p, p, p, p, dt_raw, conv_w, conv_w, conv_w, conv_b, conv_b, conv_b,
      dtb_row, alog_row, dskip_x, nrm_row, e01, selt)


def _half_rms_scale(x, low):
    sq = x * x
    zero = jnp.zeros_like(sq)
    ss_lo = jnp.sum(jnp.where(low, sq, zero), axis=-1, keepdims=True)
    ss_hi = jnp.sum(jnp.where(low, zero, sq), axis=-1, keepdims=True)
    inv = 1.0 / ATTN_HEAD_DIM
    return jnp.where(low, lax.rsqrt(ss_lo * inv + EPS), lax.rsqrt(ss_hi * inv + EPS))


RIDER_ROWS = 16


def _rider_span(rows, n_steps, max_span):
    span = 1
    while (rows * span) % (n_steps * RIDER_ROWS) != 0:
        span *= 2
        assert span <= max_span, (rows, n_steps)
    return span


def _attn_kernel(sink_ref, slope_ref, q_ref, kc_ref, kp_ref, vc_ref, vp_ref, qn_ref, kn_ref, *refs, rider_spans):
    nr = len(rider_spans)
    r_refs, o_ref, q_refs = refs[:nr], refs[nr], refs[nr + 1:]
    n = pl.program_id(1)
    blk = ATTN_BLOCK
    hd = ATTN_HEAD_DIM
    kf = jnp.concatenate([kp_ref[...], kc_ref[...]], axis=0).astype(F32)
    vf = jnp.concatenate([vp_ref[...], vc_ref[...]], axis=0).astype(F32)
    sj = lax.broadcasted_iota(jnp.int32, (2 * blk, blk), 0)
    qi = lax.broadcasted_iota(jnp.int32, (2 * blk, blk), 1)
    dist = qi + blk - sj
    valid = (dist >= 0) & (dist < WINDOW) & ((sj >= blk) | (n > 0))
    ndm = jnp.where(valid, -dist.astype(F32), NEG)
    npair = ATTN_Q_PER_KV // 2
    ndm4 = jnp.concatenate([ndm] * npair, axis=1)
    low_k = lax.broadcasted_iota(jnp.int32, (2 * blk, LANES), 1) < hd
    low_q = lax.broadcasted_iota(jnp.int32, (npair * blk, LANES), 1) < hd
    qn = qn_ref[...] * (LOG2E * ATTN_HEAD_DIM ** -0.5)
    kn = kn_ref[...]
    zk = jnp.zeros((2 * blk, LANES), F32)
    for kvp in range(ATTN_N_KV // 2):
        k2 = kf[:, kvp * LANES:(kvp + 1) * LANES]
        k2 = k2 * _half_rms_scale(k2, low_k) * kn
        v2 = vf[:, kvp * LANES:(kvp + 1) * LANES]
        k2r = pltpu.roll(k2, hd, axis=1)
        v2r = pltpu.roll(v2, hd, axis=1)
        for half in range(2):
            kv = 2 * kvp + half
            k_lo_src, k_hi_src = (k2, k2r) if half == 0 else (k2r, k2)
            v_lo_src, v_hi_src = (v2, v2r) if half == 0 else (v2r, v2)
            k_lo = jnp.where(low_k, k_lo_src, zk).astype(BF16)
            k_hi = jnp.where(low_k, zk, k_hi_src).astype(BF16)
            v_lo = jnp.where(low_k, v_lo_src, zk).astype(BF16)
            v_hi = jnp.where(low_k, zk, v_hi_src).astype(BF16)
            q4 = jnp.concatenate(
                [q_ref[:, (kv * npair + j) * LANES:(kv * npair + j + 1) * LANES] for j in range(npair)],
                axis=0).astype(F32)
            qb = (q4 * _half_rms_scale(q4, low_q) * qn).astype(BF16)
            o4 = None
            for e in range(2):
                u = 2 * kv + e
                st = _dot_nt(k_lo if e == 0 else k_hi, qb) + slope_ref[u:u + 1, :] * ndm4
                sink = sink_ref[u:u + 1, :] * LOG2E
                m = jnp.maximum(jnp.max(st, axis=0, keepdims=True), sink)
                p = jnp.exp2(st - m)
                denom = jnp.sum(p, axis=0, keepdims=True) + jnp.exp2(sink - m)
                pn = (p * (1.0 / denom)).astype(BF16)
                oe = lax.dot_general(pn, v_lo if e == 0 else v_hi, (((0,), (0,)), ((), ())),
                                     preferred_element_type=F32)
                o4 = oe if o4 is None else o4 + oe
            for j in range(npair):
                hp = kv * npair + j
                o_ref[:, hp * LANES:(hp + 1) * LANES] = o4[j * blk:(j + 1) * blk, :].astype(BF16)

    for r_ref, q_ref, span in zip(r_refs, q_refs, rider_spans):
        if span == 1:
            q_ref[...] = r_ref[...].astype(BF16)
        else:
            @pl.when(n % span == 0)
            def _(r_ref=r_ref, q_ref=q_ref):
                q_ref[...] = r_ref[...].astype(BF16)


def _head_table(per_head):
    npair = ATTN_Q_PER_KV // 2
    t = per_head.astype(F32).reshape(ATTN_N_KV, npair, 2).transpose(0, 2, 1)
    return jnp.repeat(t.reshape(2 * ATTN_N_KV, npair), LANES, axis=1)


def _attn(p, sinks, qn_row, kn_row, riders, bsz, t_len):
    blk = ATTN_BLOCK
    slopes = jnp.asarray(
        [2.0 ** (-8.0 * (h + 1) / ATTN_N_HEADS) for h in range(ATTN_N_HEADS)], F32)
    nb = t_len // blk
    cur = lambda b, n: b * nb + n
    prev = lambda b, n: b * nb + jnp.maximum(n - 1, 0)
    n_steps = bsz * nb
    spans = tuple(_rider_span(r.shape[0], n_steps, nb) for r in riders)
    rider_specs = [
        pl.BlockSpec((r.shape[0] * span // n_steps, r.shape[1]), lambda b, n, span=span: (cur(b, n) // span, 0))
        for r, span in zip(riders, spans)]
    outs = pl.pallas_call(
        functools.partial(_attn_kernel, rider_spans=spans),
        grid=(bsz, nb),
        in_specs=[
            pl.BlockSpec((2 * ATTN_N_KV, 4 * LANES), lambda b, n: (0, 0)),
            pl.BlockSpec((2 * ATTN_N_KV, 4 * LANES), lambda b, n: (0, 0)),
            pl.BlockSpec((blk, ATTN_D), lambda b, n: (cur(b, n), PB_Q // ATTN_D)),
            pl.BlockSpec((blk, ATTN_KV_D), lambda b, n: (cur(b, n), PB_K // ATTN_KV_D)),
            pl.BlockSpec((blk, ATTN_KV_D), lambda b, n: (prev(b, n), PB_K // ATTN_KV_D)),
            pl.BlockSpec((blk, ATTN_KV_D), lambda b, n: (cur(b, n), PB_V // ATTN_KV_D)),
            pl.BlockSpec((blk, ATTN_KV_D), lambda b, n: (prev(b, n), PB_V // ATTN_KV_D)),
            pl.BlockSpec((1, LANES), lambda b, n: (0, 0)),
            pl.BlockSpec((1, LANES), lambda b, n: (0, 0)),
        ] + rider_specs,
        out_specs=[pl.BlockSpec((blk, ATTN_D), lambda b, n: (cur(b, n), 0))] + rider_specs,
        out_shape=[jax.ShapeDtypeStruct((bsz * t_len, ATTN_D), BF16)]
        + [jax.ShapeDtypeStruct(r.shape, BF16) for r in riders],
        compiler_params=_params(("parallel", "arbitrary")),
        name="swa",
    )(_head_table(sinks), _head_table(slopes * LOG2E), p, p, p, p, p, qn_row, kn_row, *riders)
    return outs[0], outs[1:]


def _merge_kernel(ys_ref, ya_ref, gs_ref, ga_ref, wos_ref, woa_ref, o_ref):
    ms = _dot(ys_ref[...], wos_ref[...])
    ma = _dot(ya_ref[...], woa_ref[...])
    mg = (jax.nn.sigmoid(gs_ref[...].astype(F32)) * ms
          + jax.nn.sigmoid(ga_ref[...].astype(F32)) * ma)
    o_ref[...] = mg.astype(BF16)


def _merge(ys, ya, p, wos, woa, tm, tn):
    m = ys.shape[0]
    d = wos.shape[1]
    return pl.pallas_call(
        _merge_kernel,
        grid=(m // tm, d // tn),
        in_specs=[
            pl.BlockSpec((tm, SSM_D_INNER), lambda i, j: (i, 0)),
            pl.BlockSpec((tm, ATTN_D), lambda i, j: (i, 0)),
            pl.BlockSpec((tm, tn), lambda i, j: (i, PB_GS // tn + j)),
            pl.BlockSpec((tm, tn), lambda i, j: (i, PB_GA // tn + j)),
            pl.BlockSpec((SSM_D_INNER, tn), lambda i, j: (0, j)),
            pl.BlockSpec((ATTN_D, tn), lambda i, j: (0, j)),
        ],
        out_specs=pl.BlockSpec((tm, tn), lambda i, j: (i, j)),
        out_shape=jax.ShapeDtypeStruct((m, d), BF16),
        compiler_params=_params(("parallel", "arbitrary")),
        name="merge",
    )(ys, ya, p, p, wos, woa)


def _outproj_kernel(x_ref, m_ref, w_ref, o_ref, wb_ref):
    @pl.when(pl.program_id(0) == 0)
    def _():
        wb_ref[...] = w_ref[...].astype(BF16)

    o_ref[...] = x_ref[...] + _dot(m_ref[...], wb_ref[...])


def _outproj(x, mg, wout, tm):
    m, d = x.shape
    return pl.pallas_call(
        _outproj_kernel,
        grid=(m // tm,),
        in_specs=[
            pl.BlockSpec((tm, d), lambda i: (i, 0)),
            pl.BlockSpec((tm, d), lambda i: (i, 0)),
            pl.BlockSpec((d, d), lambda i: (0, 0), pipeline_mode=pl.Buffered(1)),
        ],
        out_specs=pl.BlockSpec((tm, d), lambda i: (i, 0)),
        out_shape=jax.ShapeDtypeStruct((m, d), F32),
        scratch_shapes=[pltpu.VMEM((d, d), BF16)],
        compiler_params=_params(("arbitrary",)),
        name="out_proj",
    )(x, mg, wout)


def _selection_constants():
    e = np.zeros((SSM_N_GROUPS, LANES, SSM_GROUP_WIDTH), np.float32)
    st = np.zeros((SSM_N_GROUPS, SSM_HEADS_PER_GROUP, LANES), np.float32)
    for g in range(SSM_N_GROUPS):
        for h in range(SSM_HEADS_PER_GROUP):
            e[g, g * SSM_HEADS_PER_GROUP + h, h * SSM_HEAD_DIM:(h + 1) * SSM_HEAD_DIM] = 1.0
            st[g, h, g * SSM_HEADS_PER_GROUP + h] = 1.0
    return jnp.asarray(e, BF16), jnp.asarray(st, BF16)


def _pad_lanes(v):
    return jnp.pad(v.astype(F32), (0, LANES - v.shape[0])).reshape(1, LANES)


def kernel(x, ffn1_norm, ffn1_w_gate, ffn1_w_up, ffn1_w_down, mix_norm, w_in, conv_w, conv_b, dt_bias, a_log, d_skip, ssm_norm, q_norm, k_norm, sinks, w_o_ssm, w_o_attn, w_out, ffn2_norm, ffn2_w_gate, ffn2_w_up, ffn2_w_down):
    bsz, t_len, d = x.shape
    m = bsz * t_len
    depth = ffn1_norm.shape[0]
    e01, selt = _selection_constants()
    xf = x.reshape(m, d)
    for l in range(depth):
        wit = jnp.swapaxes(w_in[l], 0, 1)

        xf, wpt = _ffn(xf, ffn1_norm[l].reshape(1, d), ffn1_w_gate[l].astype(BF16),
                       ffn1_w_up[l].astype(BF16), ffn1_w_down[l].astype(BF16), tm=TM, tf=FFN_TF, wit=wit)

        p, dt_raw = _inproj(xf, mix_norm[l].reshape(1, d), wpt, wit, tm=TM, tn=INPROJ_TN)

        y_ssm = _ssd(
            p, dt_raw, conv_w[l], conv_b[l].reshape(1, -1),
            _pad_lanes(dt_bias[l]), _pad_lanes(a_log[l]),
            jnp.repeat(d_skip[l].astype(F32), SSM_HEAD_DIM).reshape(1, SSM_D_INNER),
            ssm_norm[l].reshape(1, SSM_D_INNER), e01, selt, bsz, t_len)

        y_attn, (wos, woa, wg2, wu2, wd2) = _attn(
            p, sinks[l].astype(F32),
            jnp.tile(q_norm[l].astype(F32), LANES // ATTN_HEAD_DIM).reshape(1, LANES),
            jnp.tile(k_norm[l].astype(F32), LANES // ATTN_HEAD_DIM).reshape(1, LANES),
            (w_o_ssm[l], w_o_attn[l], ffn2_w_gate[l], ffn2_w_up[l], ffn2_w_down[l]), bsz, t_len)

        mg = _merge(y_ssm, y_attn, p, wos, woa, tm=TM, tn=MERGE_TN)
        xf = _outproj(xf, mg, w_out[l], tm=OUTPROJ_TM)

        xf = _ffn(xf, ffn2_norm[l].reshape(1, d), wg2, wu2, wd2, tm=TM, tf=FFN_TF)
    return xf.reshape(bsz, t_len, d)
```

```python
import functools

import jax
import jax.numpy as jnp
import numpy as np
from jax import lax
from jax.experimental import pallas as pl
from jax.experimental.pallas import tpu as pltpu

F32 = jnp.float32
BF16 = jnp.bfloat16

D_MODEL = 2048
SSM_D_INNER = 4096
SSM_HEAD_DIM = 64
SSM_N_HEADS = 64
SSM_N_GROUPS = 8
SSM_HEADS_PER_GROUP = SSM_N_HEADS // SSM_N_GROUPS
SSM_GROUP_WIDTH = SSM_D_INNER // SSM_N_GROUPS
SSM_D_STATE = 128
SSM_CONV = 4
SSM_CHUNK = 256
SSM_GN = SSM_N_GROUPS * SSM_D_STATE
ATTN_HEAD_DIM = 64
ATTN_N_HEADS = 32
ATTN_N_KV = 4
ATTN_Q_PER_KV = 8
ATTN_D = ATTN_N_HEADS * ATTN_HEAD_DIM
ATTN_KV_D = ATTN_N_KV * ATTN_HEAD_DIM
WINDOW = 128
ATTN_BLOCK = 128
D_FF = 5632
EPS = 1e-6
NEG = -1e30
LOG2E = 1.4426950408889634

LANES = 128
CONV_HALO = 8

PA_Z = 0
PA_X = PA_Z + SSM_D_INNER
PA_B = PA_X + SSM_D_INNER
PA_C = PA_B + SSM_GN
PA_COLS = PA_C + SSM_GN
PB_Q = PA_COLS
PB_K = PB_Q + ATTN_D
PB_V = PB_K + ATTN_KV_D
PB_GS = PB_V + ATTN_KV_D
PB_GA = PB_GS + D_MODEL
P_COLS = PB_GA + D_MODEL

W_DT = PA_COLS
W_Q = W_DT + SSM_N_HEADS

VMEM_LIMIT = 56 * 1024 * 1024

TM = 1024
FFN_TF = 512
INPROJ_TN = 1536
MERGE_TN = 512
OUTPROJ_TM = 512


def _params(sem):
    return pltpu.CompilerParams(dimension_semantics=sem, vmem_limit_bytes=VMEM_LIMIT)


def _rms(x, gain):
    return x * lax.rsqrt(jnp.mean(x * x, axis=-1, keepdims=True) + EPS) * gain


def _silu(x):
    return x * jax.nn.sigmoid(x)


def _dot(a, b):
    return jnp.dot(a, b, preferred_element_type=F32)


def _dot_nt(a, b):
    return lax.dot_general(a, b, (((1,), (1,)), ((), ())), preferred_element_type=F32)


def _split3(x):
    hi = x.astype(BF16)
    r1 = x - hi.astype(F32)
    mid = r1.astype(BF16)
    lo = (r1 - mid.astype(F32)).astype(BF16)
    return hi, mid, lo


def _sel_left(m01, x):
    hi, mid, lo = _split3(x)
    return (_dot(m01, lo) + _dot(m01, mid)) + _dot(m01, hi)


def _sel_right(x, m01):
    hi, mid, lo = _split3(x)
    return (_dot(lo, m01) + _dot(mid, m01)) + _dot(hi, m01)


PACK_ROWS = 256


def _ffn_kernel(x_ref, gain_ref, wg_ref, wu_ref, wd_ref, *refs, n_pack):
    if n_pack:
        wi_ref, o_ref, wp_ref, h_ref = refs
    else:
        o_ref, h_ref = refs
    i = pl.program_id(0)
    j = pl.program_id(1)

    @pl.when(j == 0)
    def _():
        x = x_ref[...]
        h_ref[...] = _rms(x, gain_ref[...]).astype(BF16)
        o_ref[...] = x

    h = h_ref[...]
    g = _dot(h, wg_ref[...])
    u = _dot(h, wu_ref[...])
    a = (0.5 * _silu(g) * u).astype(BF16)
    o_ref[...] += _dot(a, wd_ref[...])

    if n_pack:
        @pl.when(i * pl.num_programs(1) + j < n_pack)
        def _():
            wp_ref[...] = wi_ref[...].astype(BF16)


def _ffn(x, gain, wg, wu, wd, tm, tf, wit=None):
    m, d = x.shape
    dff = wg.shape[1]
    nj = dff // tf
    in_specs = [
        pl.BlockSpec((tm, d), lambda i, j: (i, 0)),
        pl.BlockSpec((1, d), lambda i, j: (0, 0)),
        pl.BlockSpec((d, tf), lambda i, j: (0, j)),
        pl.BlockSpec((d, tf), lambda i, j: (0, j)),
        pl.BlockSpec((tf, d), lambda i, j: (j, 0)),
    ]
    out_specs = [pl.BlockSpec((tm, d), lambda i, j: (i, 0))]
    out_shape = [jax.ShapeDtypeStruct((m, d), F32)]
    operands = [x, gain, wg, wu, wd]
    n_pack = 0
    if wit is not None:
        n_pack = P_COLS // PACK_ROWS
        assert n_pack <= (m // tm) * nj and PA_COLS % PACK_ROWS == 0
        blk = lambda i, j: jnp.minimum(i * nj + j, n_pack - 1)
        in_specs[0] = pl.BlockSpec((tm, d), lambda i, j: (i, 0), pipeline_mode=pl.Buffered(1))
        in_specs.append(pl.BlockSpec(
            (pl.Element(PACK_ROWS), pl.Element(d)),
            lambda i, j: (pl.multiple_of(
                blk(i, j) * PACK_ROWS + jnp.where(blk(i, j) * PACK_ROWS >= PA_COLS, SSM_N_HEADS, 0),
                SSM_N_HEADS), 0)))
        out_specs.append(pl.BlockSpec((PACK_ROWS, d), lambda i, j: (blk(i, j), 0)))
        out_shape.append(jax.ShapeDtypeStruct((P_COLS, d), BF16))
        operands.append(wit)
    outs = pl.pallas_call(
        functools.partial(_ffn_kernel, n_pack=n_pack),
        grid=(m // tm, nj),
        in_specs=in_specs,
        out_specs=out_specs,
        out_shape=out_shape,
        scratch_shapes=[pltpu.VMEM((tm, d), BF16)],
        compiler_params=_params(("arbitrary" if n_pack else "parallel", "arbitrary")),
        name="ffn",
    )(*operands)
    return outs if n_pack else outs[0]


def _inproj_kernel(x_ref, gain_ref, w_ref, wdt_ref, p_ref, dt_ref, h_ref):
    j = pl.program_id(1)

    @pl.when(j == 0)
    def _():
        h = _rms(x_ref[...], gain_ref[...]).astype(BF16)
        h_ref[...] = h
        dt = _dot_nt(h, wdt_ref[...].astype(BF16))
        lane = lax.broadcasted_iota(jnp.int32, dt.shape, 1)
        dt_ref[...] = jnp.where(lane < SSM_N_HEADS, dt, 0.0)

    p_ref[...] = _dot_nt(h_ref[...], w_ref[...]).astype(BF16)


def _inproj(x, gain, wpt, wit, tm, tn):
    m, d = x.shape
    return pl.pallas_call(
        _inproj_kernel,
        grid=(m // tm, P_COLS // tn),
        in_specs=[
            pl.BlockSpec((tm, d), lambda i, j: (i, 0)),
            pl.BlockSpec((1, d), lambda i, j: (0, 0)),
            pl.BlockSpec((tn, d), lambda i, j: (j, 0)),
            pl.BlockSpec((LANES, d), lambda i, j: (W_DT // LANES, 0)),
        ],
        out_specs=[
            pl.BlockSpec((tm, tn), lambda i, j: (i, j)),
            pl.BlockSpec((tm, LANES), lambda i, j: (i, 0)),
        ],
        out_shape=[
            jax.ShapeDtypeStruct((m, P_COLS), BF16),
            jax.ShapeDtypeStruct((m, LANES), F32),
        ],
        scratch_shapes=[pltpu.VMEM((tm, d), BF16)],
        compiler_params=_params(("parallel", "arbitrary")),
        name="in_proj",
    )(x, gain, wpt, wit)


SSD_GROUPS_PER_STEP = 8


def _ssd_kernel(z_ref, x_ref, b_ref, c_ref, dt_ref,
                cwx_ref, cwb_ref, cwc_ref, cbx_ref, cbb_ref, cbc_ref,
                dtb_ref, alog_ref, dsk_ref, nrm_ref, e_ref, selt_ref,
                o_ref, ext_ref, state_ref):
    GW = SSM_GROUP_WIDTH
    N = SSM_D_STATE
    for gi in range(SSD_GROUPS_PER_STEP):
        def part(ref, width, gi=gi):
            return ref.at[:, gi * width:(gi + 1) * width]
        _ssd_group(part(z_ref, GW), part(x_ref, GW), part(b_ref, N), part(c_ref, N), dt_ref,
                   part(cwx_ref, GW), part(cwb_ref, N), part(cwc_ref, N),
                   part(cbx_ref, GW), part(cbb_ref, N), part(cbc_ref, N),
                   dtb_ref, alog_ref, part(dsk_ref, GW), part(nrm_ref, GW), e_ref.at[gi], selt_ref.at[gi],
                   part(o_ref, GW), ext_ref.at[gi], state_ref.at[gi])


def _ssd_group(z_ref, x_ref, b_ref, c_ref, dt_ref,
               cwx_ref, cwb_ref, cwc_ref, cbx_ref, cbb_ref, cbc_ref,
               dtb_ref, alog_ref, dsk_ref, nrm_ref, e_ref, selt_ref,
               o_ref, ext_ref, state_ref):
    c = pl.program_id(2)
    L = SSM_CHUNK
    GW = SSM_GROUP_WIDTH
    N = SSM_D_STATE
    XBC = GW + 2 * N

    @pl.when(c == 0)
    def _():
        ext_ref[0:CONV_HALO, :] = jnp.zeros((CONV_HALO, XBC), F32)
        state_ref[...] = jnp.zeros_like(state_ref)

    ext_ref[CONV_HALO:CONV_HALO + L, 0:GW] = x_ref[...].astype(F32)
    ext_ref[CONV_HALO:CONV_HALO + L, GW:GW + N] = b_ref[...].astype(F32)
    ext_ref[CONV_HALO:CONV_HALO + L, GW + N:XBC] = c_ref[...].astype(F32)

    cw = jnp.concatenate([cwx_ref[...], cwb_ref[...], cwc_ref[...]], axis=1)
    cbias = jnp.concatenate([cbx_ref[...], cbb_ref[...], cbc_ref[...]], axis=1)
    ext = ext_ref[...]
    acc = cw[0:1, :] * ext
    for k in range(1, SSM_CONV):
        acc = pltpu.roll(acc, 1, axis=0) + cw[k:k + 1, :] * ext
    ext_ref[0:CONV_HALO, :] = ext_ref[L:L + CONV_HALO, :]
    xbc = _silu(acc[CONV_HALO:, :] + cbias)
    xs = xbc[:, 0:GW]
    bm = xbc[:, GW:GW + N]
    cm = xbc[:, GW + N:XBC]

    dtr = dt_ref[...] + dtb_ref[...]
    dt = jnp.maximum(dtr, 0.0) + jnp.log1p(jnp.exp(-jnp.abs(dtr)))
    e01 = e_ref[...]
    dt_x = _sel_right(dt, e01)
    adt_c = dt * (-LOG2E * jnp.exp(alog_ref[...]))
    row = lax.broadcasted_iota(jnp.int32, (L, L), 0)
    col = lax.broadcasted_iota(jnp.int32, (L, L), 1)
    tril = row >= col
    tril01 = tril.astype(BF16)
    acum_c = _sel_left(tril01, adt_c)
    selt = selt_ref[...]
    hi, mid, lo = _split3(acum_c)
    acum_r = (_dot_nt(selt, lo) + _dot_nt(selt, mid)) + _dot_nt(selt, hi)
    acum_x = (_dot(lo, e01) + _dot(mid, e01)) + _dot(hi, e01)

    bt = bm.T.astype(BF16)
    cmb = cm.astype(BF16)
    H = L // 2
    cb_top = _dot(cmb[0:H, :], bt[:, 0:H])
    cb_bot = _dot(cmb[H:L, :], bt)
    xdt = xs * dt_x
    xdt_b = xdt.astype(BF16)

    tri = tril[0:H, 0:H]
    lane = lax.broadcasted_iota(jnp.int32, (L, LANES), 1)
    low_half = lane < SSM_HEAD_DIM
    pairs = []
    for pair in range(SSM_HEADS_PER_GROUP // 2):
        xp = xdt_b[:, pair * LANES:(pair + 1) * LANES]
        y_top = None
        y_bot = None
        for half in range(2):
            h = 2 * pair + half
            a_col = acum_x[:, h * SSM_HEAD_DIM:h * SSM_HEAD_DIM + 1]
            a_row = acum_r[h:h + 1, :]
            w00 = cb_top * jnp.exp2(jnp.where(tri, a_col[0:H] - a_row[:, 0:H], NEG))
            w10 = cb_bot[:, 0:H] * jnp.exp2(a_col[H:L] - a_row[:, 0:H])
            w11 = cb_bot[:, H:L] * jnp.exp2(jnp.where(tri, a_col[H:L] - a_row[:, H:L], NEG))
            keep = low_half if half == 0 else jnp.logical_not(low_half)
            xh = jnp.where(keep, xp, jnp.zeros_like(xp))
            yt = _dot(w00.astype(BF16), xh[0:H, :])
            yb = _dot(jnp.concatenate([w10, w11], axis=1).astype(BF16), xh)
            y_top = yt if y_top is None else y_top + yt
            y_bot = yb if y_bot is None else y_bot + yb
        pairs.append(jnp.concatenate([y_top, y_bot], axis=0))
    y_diag = jnp.concatenate(pairs, axis=1)

    state = state_ref[...]
    y_off = jnp.exp2(acum_x) * _dot(cmb, state.astype(BF16))
    y = y_diag + y_off + dsk_ref[...] * xs

    last = acum_x[L - 1:L, :]
    xdec = (xdt * jnp.exp2(last - acum_x)).astype(BF16)
    state_ref[...] = state * jnp.exp2(last) + _dot(bt, xdec)

    yg = y * _silu(z_ref[...].astype(F32))
    o_ref[...] = _rms(yg, nrm_ref[...]).astype(BF16)


def _ssd(p, dt_raw, conv_w, conv_b, dtb_row, alog_row, dskip_x, nrm_row, e01, selt, bsz, t_len):
    L = SSM_CHUNK
    nc = t_len // L
    G = SSD_GROUPS_PER_STEP
    GW = G * SSM_GROUP_WIDTH
    N = G * SSM_D_STATE
    rowblk = lambda b, g, c: b * nc + c
    return pl.pallas_call(
        _ssd_kernel,
        grid=(bsz, SSM_N_GROUPS // G, nc),
        in_specs=[
            pl.BlockSpec((L, GW), lambda b, g, c: (rowblk(b, g, c), PA_Z // GW + g)),
            pl.BlockSpec((L, GW), lambda b, g, c: (rowblk(b, g, c), PA_X // GW + g)),
            pl.BlockSpec((L, N), lambda b, g, c: (rowblk(b, g, c), PA_B // N + g)),
            pl.BlockSpec((L, N), lambda b, g, c: (rowblk(b, g, c), PA_C // N + g)),
            pl.BlockSpec((L, LANES), lambda b, g, c: (rowblk(b, g, c), 0)),
            pl.BlockSpec((SSM_CONV, GW), lambda b, g, c: (0, g)),
            pl.BlockSpec((SSM_CONV, N), lambda b, g, c: (0, SSM_D_INNER // N + g)),
            pl.BlockSpec((SSM_CONV, N), lambda b, g, c: (0, (SSM_D_INNER + SSM_GN) // N + g)),
            pl.BlockSpec((1, GW), lambda b, g, c: (0, g)),
            pl.BlockSpec((1, N), lambda b, g, c: (0, SSM_D_INNER // N + g)),
            pl.BlockSpec((1, N), lambda b, g, c: (0, (SSM_D_INNER + SSM_GN) // N + g)),
            pl.BlockSpec((1, LANES), lambda b, g, c: (0, 0)),
            pl.BlockSpec((1, LANES), lambda b, g, c: (0, 0)),
            pl.BlockSpec((1, GW), lambda b, g, c: (0, g)),
            pl.BlockSpec((1, GW), lambda b, g, c: (0, g)),
            pl.BlockSpec((G, LANES, SSM_GROUP_WIDTH), lambda b, g, c: (g, 0, 0)),
            pl.BlockSpec((G, SSM_HEADS_PER_GROUP, LANES), lambda b, g, c: (g, 0, 0)),
        ],
        out_specs=pl.BlockSpec((L, GW), lambda b, g, c: (rowblk(b, g, c), g)),
        out_shape=jax.ShapeDtypeStruct((bsz * t_len, SSM_D_INNER), BF16),
        scratch_shapes=[
            pltpu.VMEM((G, CONV_HALO + L, SSM_GROUP_WIDTH + 2 * SSM_D_STATE), F32),
            pltpu.VMEM((G, SSM_D_STATE, SSM_GROUP_WIDTH), F32),
        ],
        compiler_params=_params(("parallel", "parallel", "arbitrary")),
        name="ssd",
    )(p, p, p, p, dt_raw, conv_w, conv_w, conv_w, conv_b, conv_b, conv_b,
      dtb_row, alog_row, dskip_x, nrm_row, e01, selt)


def _half_rms_scale(x, low):
    sq = x * x
    zero = jnp.zeros_like(sq)
    ss_lo = jnp.sum(jnp.where(low, sq, zero), axis=-1, keepdims=True)
    ss_hi = jnp.sum(jnp.where(low, zero, sq), axis=-1, keepdims=True)
    inv = 1.0 / ATTN_HEAD_DIM
    return jnp.where(low, lax.rsqrt(ss_lo * inv + EPS), lax.rsqrt(ss_hi * inv + EPS))


RIDER_ROWS = 16


def _rider_span(rows, n_steps, max_span):
    span = 1
    while (rows * span) % (n_steps * RIDER_ROWS) != 0:
        span *= 2
        assert span <= max_span, (rows, n_steps)
    return span


ATTN_BLOCKS_PER_STEP = 2


def _attn_block(has_prev, r0, kf, vf, sink_ref, slope_ref, q_ref, qn_ref, kn_ref, o_ref):
    blk = ATTN_BLOCK
    hd = ATTN_HEAD_DIM
    sj = lax.broadcasted_iota(jnp.int32, (2 * blk, blk), 0)
    qi = lax.broadcasted_iota(jnp.int32, (2 * blk, blk), 1)
    dist = qi + blk - sj
    valid = (dist >= 0) & (dist < WINDOW)
    if has_prev is not True:
        valid = valid & ((sj >= blk) | has_prev)
    ndm = jnp.where(valid, -dist.astype(F32), NEG)
    npair = ATTN_Q_PER_KV // 2
    ndm4 = jnp.concatenate([ndm] * npair, axis=1)
    low_k = lax.broadcasted_iota(jnp.int32, (2 * blk, LANES), 1) < hd
    low_q = lax.broadcasted_iota(jnp.int32, (npair * blk, LANES), 1) < hd
    qn = qn_ref[...] * (LOG2E * ATTN_HEAD_DIM ** -0.5)
    kn = kn_ref[...]
    zk = jnp.zeros((2 * blk, LANES), F32)
    for kvp in range(ATTN_N_KV // 2):
        k2 = kf[:, kvp * LANES:(kvp + 1) * LANES]
        k2 = k2 * _half_rms_scale(k2, low_k) * kn
        v2 = vf[:, kvp * LANES:(kvp + 1) * LANES]
        k2r = pltpu.roll(k2, hd, axis=1)
        v2r = pltpu.roll(v2, hd, axis=1)
        for half in range(2):
            kv = 2 * kvp + half
            k_lo_src, k_hi_src = (k2, k2r) if half == 0 else (k2r, k2)
            v_lo_src, v_hi_src = (v2, v2r) if half == 0 else (v2r, v2)
            k_lo = jnp.where(low_k, k_lo_src, zk).astype(BF16)
            k_hi = jnp.where(low_k, zk, k_hi_src).astype(BF16)
            v_lo = jnp.where(low_k, v_lo_src, zk).astype(BF16)
            v_hi = jnp.where(low_k, zk, v_hi_src).astype(BF16)
            q4 = jnp.concatenate(
                [q_ref[r0:r0 + blk, (kv * npair + j) * LANES:(kv * npair + j + 1) * LANES] for j in range(npair)],
                axis=0).astype(F32)
            qb = (q4 * _half_rms_scale(q4, low_q) * qn).astype(BF16)
            o4 = None
            for e in range(2):
                u = 2 * kv + e
                st = _dot_nt(k_lo if e == 0 else k_hi, qb) + slope_ref[u:u + 1, :] * ndm4
                sink = sink_ref[u:u + 1, :] * LOG2E
                m = jnp.maximum(jnp.max(st, axis=0, keepdims=True), sink)
                p = jnp.exp2(st - m)
                denom = jnp.sum(p, axis=0, keepdims=True) + jnp.exp2(sink - m)
                pn = (p * (1.0 / denom)).astype(BF16)
                oe = lax.dot_general(pn, v_lo if e == 0 else v_hi, (((0,), (0,)), ((), ())),
                                     preferred_element_type=F32)
                o4 = oe if o4 is None else o4 + oe
            for j in range(npair):
                hp = kv * npair + j
                o_ref[r0:r0 + blk, hp * LANES:(hp + 1) * LANES] = o4[j * blk:(j + 1) * blk, :].astype(BF16)


def _attn_kernel(sink_ref, slope_ref, q_ref, kc_ref, kp_ref, vc_ref, vp_ref, qn_ref, kn_ref, *refs, rider_spans):
    nr = len(rider_spans)
    r_refs, o_ref, q_refs = refs[:nr], refs[nr], refs[nr + 1:]
    n = pl.program_id(1)
    blk = ATTN_BLOCK
    k_all = jnp.concatenate([kp_ref[...], kc_ref[...]], axis=0).astype(F32)
    v_all = jnp.concatenate([vp_ref[...], vc_ref[...]], axis=0).astype(F32)
    for sub in range(ATTN_BLOCKS_PER_STEP):
        has_prev = (n > 0) if sub == 0 else True
        _attn_block(has_prev, sub * blk, k_all[sub * blk:(sub + 2) * blk, :], v_all[sub * blk:(sub + 2) * blk, :],
                    sink_ref, slope_ref, q_ref, qn_ref, kn_ref, o_ref)

    for r_ref, q_ref, span in zip(r_refs, q_refs, rider_spans):
        if span == 1:
            q_ref[...] = r_ref[...].astype(BF16)
        else:
            @pl.when(n % span == 0)
            def _(r_ref=r_ref, q_ref=q_ref):
                q_ref[...] = r_ref[...].astype(BF16)


def _head_table(per_head):
    npair = ATTN_Q_PER_KV // 2
    t = per_head.astype(F32).reshape(ATTN_N_KV, npair, 2).transpose(0, 2, 1)
    return jnp.repeat(t.reshape(2 * ATTN_N_KV, npair), LANES, axis=1)


def _attn(p, sinks, qn_row, kn_row, riders, bsz, t_len):
    blk = ATTN_BLOCK
    slopes = jnp.asarray(
        [2.0 ** (-8.0 * (h + 1) / ATTN_N_HEADS) for h in range(ATTN_N_HEADS)], F32)
    per = ATTN_BLOCKS_PER_STEP
    nb = t_len // (per * blk)
    cur = lambda b, n: b * nb + n
    prev = lambda b, n: b * nb * per + jnp.maximum(n * per - 1, 0)
    n_steps = bsz * nb
    spans = tuple(_rider_span(r.shape[0], n_steps, nb) for r in riders)
    rider_specs = [
        pl.BlockSpec((r.shape[0] * span // n_steps, r.shape[1]), lambda b, n, span=span: (cur(b, n) // span, 0))
        for r, span in zip(riders, spans)]
    outs = pl.pallas_call(
        functools.partial(_attn_kernel, rider_spans=spans),
        grid=(bsz, nb),
        in_specs=[
            pl.BlockSpec((2 * ATTN_N_KV, 4 * LANES), lambda b, n: (0, 0)),
            pl.BlockSpec((2 * ATTN_N_KV, 4 * LANES), lambda b, n: (0, 0)),
            pl.BlockSpec((per * blk, ATTN_D), lambda b, n: (cur(b, n), PB_Q // ATTN_D)),
            pl.BlockSpec((per * blk, ATTN_KV_D), lambda b, n: (cur(b, n), PB_K // ATTN_KV_D)),
            pl.BlockSpec((blk, ATTN_KV_D), lambda b, n: (prev(b, n), PB_K // ATTN_KV_D)),
            pl.BlockSpec((per * blk, ATTN_KV_D), lambda b, n: (cur(b, n), PB_V // ATTN_KV_D)),
            pl.BlockSpec((blk, ATTN_KV_D), lambda b, n: (prev(b, n), PB_V // ATTN_KV_D)),
            pl.BlockSpec((1, LANES), lambda b, n: (0, 0)),
            pl.BlockSpec((1, LANES), lambda b, n: (0, 0)),
        ] + rider_specs,
        out_specs=[pl.BlockSpec((per * blk, ATTN_D), lambda b, n: (cur(b, n), 0))] + rider_specs,
        out_shape=[jax.ShapeDtypeStruct((bsz * t_len, ATTN_D), BF16)]
        + [jax.ShapeDtypeStruct(r.shape, BF16) for r in riders],
        compiler_params=_params(("parallel", "arbitrary")),
        name="swa",
    )(_head_table(sinks), _head_table(slopes * LOG2E), p, p, p, p, p, qn_row, kn_row, *riders)
    return outs[0], outs[1:]


def _merge_kernel(ys_ref, ya_ref, gs_ref, ga_ref, wos_ref, woa_ref, o_ref):
    ms = _dot(ys_ref[...], wos_ref[...])
    ma = _dot(ya_ref[...], woa_ref[...])
    mg = (jax.nn.sigmoid(gs_ref[...].astype(F32)) * ms
          + jax.nn.sigmoid(ga_ref[...].astype(F32)) * ma)
    o_ref[...] = mg.astype(BF16)


def _merge(ys, ya, p, wos, woa, tm, tn):
    m = ys.shape[0]
    d = wos.shape[1]
    return pl.pallas_call(
        _merge_kernel,
        grid=(m // tm, d // tn),
        in_specs=[
            pl.BlockSpec((tm, SSM_D_INNER), lambda i, j: (i, 0)),
            pl.BlockSpec((tm, ATTN_D), lambda i, j: (i, 0)),
            pl.BlockSpec((tm, tn), lambda i, j: (i, PB_GS // tn + j)),
            pl.BlockSpec((tm, tn), lambda i, j: (i, PB_GA // tn + j)),
            pl.BlockSpec((SSM_D_INNER, tn), lambda i, j: (0, j)),
            pl.BlockSpec((ATTN_D, tn), lambda i, j: (0, j)),
        ],
        out_specs=pl.BlockSpec((tm, tn), lambda i, j: (i, j)),
        out_shape=jax.ShapeDtypeStruct((m, d), BF16),
        compiler_params=_params(("parallel", "arbitrary")),
        name="merge",
    )(ys, ya, p, p, wos, woa)


def _outproj_kernel(x_ref, m_ref, w_ref, o_ref, wb_ref):
    @pl.when(pl.program_id(0) == 0)
    def _():
        wb_ref[...] = w_ref[...].astype(BF16)

    o_ref[...] = x_ref[...] + _dot(m_ref[...], wb_ref[...])


def _outproj(x, mg, wout, tm):
    m, d = x.shape
    return pl.pallas_call(
        _outproj_kernel,
        grid=(m // tm,),
        in_specs=[
            pl.BlockSpec((tm, d), lambda i: (i, 0)),
            pl.BlockSpec((tm, d), lambda i: (i, 0)),
            pl.BlockSpec((d, d), lambda i: (0, 0), pipeline_mode=pl.Buffered(1)),
        ],
        out_specs=pl.BlockSpec((tm, d), lambda i: (i, 0)),
        out_shape=jax.ShapeDtypeStruct((m, d), F32),
        scratch_shapes=[pltpu.VMEM((d, d), BF16)],
        compiler_params=_params(("arbitrary",)),
        name="out_proj",
    )(x, mg, wout)


def _selection_constants():
    e = np.zeros((SSM_N_GROUPS, LANES, SSM_GROUP_WIDTH), np.float32)
    st = np.zeros((SSM_N_GROUPS, SSM_HEADS_PER_GROUP, LANES), np.float32)
    for g in range(SSM_N_GROUPS):
        for h in range(SSM_HEADS_PER_GROUP):
            e[g, g * SSM_HEADS_PER_GROUP + h, h * SSM_HEAD_DIM:(h + 1) * SSM_HEAD_DIM] = 1.0
            st[g, h, g * SSM_HEADS_PER_GROUP + h] = 1.0
    return jnp.asarray(e, BF16), jnp.asarray(st, BF16)


def _pad_lanes(v):
    return jnp.pad(v.astype(F32), (0, LANES - v.shape[0])).reshape(1, LANES)


def kernel(x, ffn1_norm, ffn1_w_gate, ffn1_w_up, ffn1_w_down, mix_norm, w_in, conv_w, conv_b, dt_bias, a_log, d_skip, ssm_norm, q_norm, k_norm, sinks, w_o_ssm, w_o_attn, w_out, ffn2_norm, ffn2_w_gate, ffn2_w_up, ffn2_w_down):
    bsz, t_len, d = x.shape
    m = bsz * t_len
    depth = ffn1_norm.shape[0]
    e01, selt = _selection_constants()
    xf = x.reshape(m, d)
    for l in range(depth):
        wit = jnp.swapaxes(w_in[l], 0, 1)

        xf, wpt = _ffn(xf, ffn1_norm[l].reshape(1, d), ffn1_w_gate[l].astype(BF16),
                       ffn1_w_up[l].astype(BF16), ffn1_w_down[l].astype(BF16), tm=TM, tf=FFN_TF, wit=wit)

        p, dt_raw = _inproj(xf, mix_norm[l].reshape(1, d), wpt, wit, tm=TM, tn=INPROJ_TN)

        y_ssm = _ssd(
            p, dt_raw, conv_w[l], conv_b[l].reshape(1, -1),
            _pad_lanes(dt_bias[l]), _pad_lanes(a_log[l]),
            jnp.repeat(d_skip[l].astype(F32), SSM_HEAD_DIM).reshape(1, SSM_D_INNER),
            ssm_norm[l].reshape(1, SSM_D_INNER), e01, selt, bsz, t_len)

        y_attn, (wos, woa, wg2, wu2, wd2) = _attn(
            p, sinks[l].astype(F32),
            jnp.tile(q_norm[l].astype(F32), LANES // ATTN_HEAD_DIM).reshape(1, LANES),
            jnp.tile(k_norm[l].astype(F32), LANES // ATTN_HEAD_DIM).reshape(1, LANES),
            (w_o_ssm[l], w_o_attn[l], ffn2_w_gate[l], ffn2_w_up[l], ffn2_w_down[l]), bsz, t_len)

        mg = _merge(y_ssm, y_attn, p, wos, woa, tm=TM, tn=MERGE_TN)
        xf = _outproj(xf, mg, w_out[l], tm=OUTPROJ_TM)

        xf = _ffn(xf, ffn2_norm[l].reshape(1, d), wg2, wu2, wd2, tm=TM, tf=FFN_TF)
    return xf.reshape(bsz, t_len, d)
```

```python
import functools

import jax
import jax.numpy as jnp
import numpy as np
from jax import lax
from jax.experimental import pallas as pl
from jax.experimental.pallas import tpu as pltpu

F32 = jnp.float32
BF16 = jnp.bfloat16

D_MODEL = 2048
SSM_D_INNER = 4096
SSM_HEAD_DIM = 64
SSM_N_HEADS = 64
SSM_N_GROUPS = 8
SSM_HEADS_PER_GROUP = SSM_N_HEADS // SSM_N_GROUPS
SSM_GROUP_WIDTH = SSM_D_INNER // SSM_N_GROUPS
SSM_D_STATE = 128
SSM_CONV = 4
SSM_CHUNK = 256
SSM_GN = SSM_N_GROUPS * SSM_D_STATE
ATTN_HEAD_DIM = 64
ATTN_N_HEADS = 32
ATTN_N_KV = 4
ATTN_Q_PER_KV = 8
ATTN_D = ATTN_N_HEADS * ATTN_HEAD_DIM
ATTN_KV_D = ATTN_N_KV * ATTN_HEAD_DIM
WINDOW = 128
ATTN_BLOCK = 128
D_FF = 5632
EPS = 1e-6
NEG = -1e30
LOG2E = 1.4426950408889634

LANES = 128
CONV_HALO = 8

PA_Z = 0
PA_X = PA_Z + SSM_D_INNER
PA_B = PA_X + SSM_D_INNER
PA_C = PA_B + SSM_GN
PA_COLS = PA_C + SSM_GN
PB_Q = PA_COLS
PB_K = PB_Q + ATTN_D
PB_V = PB_K + ATTN_KV_D
PB_GS = PB_V + ATTN_KV_D
PB_GA = PB_GS + D_MODEL
P_COLS = PB_GA + D_MODEL

W_DT = PA_COLS
W_Q = W_DT + SSM_N_HEADS

VMEM_LIMIT = 56 * 1024 * 1024

TM = 1024
FFN_TF = 512
INPROJ_TN = 1536
MERGE_TN = 512
OUTPROJ_TM = 512


def _params(sem):
    return pltpu.CompilerParams(dimension_semantics=sem, vmem_limit_bytes=VMEM_LIMIT)


def _rms(x, gain):
    return x * lax.rsqrt(jnp.mean(x * x, axis=-1, keepdims=True) + EPS) * gain


def _silu(x):
    return x * jax.nn.sigmoid(x)


def _dot(a, b):
    return jnp.dot(a, b, preferred_element_type=F32)


def _dot_nt(a, b):
    return lax.dot_general(a, b, (((1,), (1,)), ((), ())), preferred_element_type=F32)


def _split3(x):
    hi = x.astype(BF16)
    r1 = x - hi.astype(F32)
    mid = r1.astype(BF16)
    lo = (r1 - mid.astype(F32)).astype(BF16)
    return hi, mid, lo


def _sel_left(m01, x):
    hi, mid, lo = _split3(x)
    return (_dot(m01, lo) + _dot(m01, mid)) + _dot(m01, hi)


def _sel_right(x, m01):
    hi, mid, lo = _split3(x)
    return (_dot(lo, m01) + _dot(mid, m01)) + _dot(hi, m01)


PACK_ROWS = 256


def _ffn_kernel(x_ref, gain_ref, wg_ref, wu_ref, wd_ref, *refs, n_pack):
    i = pl.program_id(0)
    j = pl.program_id(1)
    if n_pack:
        wi_ref, o_ref, wp_ref, h_ref, xbuf, sem = refs
        tm = xbuf.shape[0]

        def x_copy(tile):
            return pltpu.make_async_copy(x_ref.at[pl.ds(tile * tm, tm), :], xbuf, sem)

        @pl.when((i == 0) & (j == 0))
        def _():
            x_copy(0).start()

        @pl.when(j == 0)
        def _():
            x_copy(i).wait()

        x_tile = xbuf
    else:
        o_ref, h_ref = refs
        x_tile = x_ref

    @pl.when(j == 0)
    def _():
        x = x_tile[...]
        h_ref[...] = _rms(x, gain_ref[...]).astype(BF16)
        o_ref[...] = x

    if n_pack:
        @pl.when((j == 1) & (i + 1 < pl.num_programs(0)))
        def _():
            x_copy(i + 1).start()

    h = h_ref[...]
    g = _dot(h, wg_ref[...])
    u = _dot(h, wu_ref[...])
    a = (0.5 * _silu(g) * u).astype(BF16)
    o_ref[...] += _dot(a, wd_ref[...])

    if n_pack:
        @pl.when(i * pl.num_programs(1) + j < n_pack)
        def _():
            wp_ref[...] = wi_ref[...].astype(BF16)


def _ffn(x, gain, wg, wu, wd, tm, tf, wit=None):
    m, d = x.shape
    dff = wg.shape[1]
    nj = dff // tf
    in_specs = [
        pl.BlockSpec((tm, d), lambda i, j: (i, 0)),
        pl.BlockSpec((1, d), lambda i, j: (0, 0)),
        pl.BlockSpec((d, tf), lambda i, j: (0, j)),
        pl.BlockSpec((d, tf), lambda i, j: (0, j)),
        pl.BlockSpec((tf, d), lambda i, j: (j, 0)),
    ]
    out_specs = [pl.BlockSpec((tm, d), lambda i, j: (i, 0))]
    out_shape = [jax.ShapeDtypeStruct((m, d), F32)]
    operands = [x, gain, wg, wu, wd]
    n_pack = 0
    if wit is not None:
        n_pack = P_COLS // PACK_ROWS
        assert n_pack <= (m // tm) * nj and PA_COLS % PACK_ROWS == 0
        blk = lambda i, j: jnp.minimum(i * nj + j, n_pack - 1)
        in_specs[0] = pl.BlockSpec(memory_space=pl.ANY)
        in_specs.append(pl.BlockSpec(
            (pl.Element(PACK_ROWS), pl.Element(d)),
            lambda i, j: (pl.multiple_of(
                blk(i, j) * PACK_ROWS + jnp.where(blk(i, j) * PACK_ROWS >= PA_COLS, SSM_N_HEADS, 0),
                SSM_N_HEADS), 0)))
        out_specs.append(pl.BlockSpec((PACK_ROWS, d), lambda i, j: (blk(i, j), 0)))
        out_shape.append(jax.ShapeDtypeStruct((P_COLS, d), BF16))
        operands.append(wit)
    outs = pl.pallas_call(
        functools.partial(_ffn_kernel, n_pack=n_pack),
        grid=(m // tm, nj),
        in_specs=in_specs,
        out_specs=out_specs,
        out_shape=out_shape,
        scratch_shapes=[pltpu.VMEM((tm, d), BF16)]
        + ([pltpu.VMEM((tm, d), F32), pltpu.SemaphoreType.DMA(())] if n_pack else []),
        compiler_params=_params(("arbitrary" if n_pack else "parallel", "arbitrary")),
        name="ffn",
    )(*operands)
    return outs if n_pack else outs[0]


def _inproj_kernel(x_ref, gain_ref, w_ref, wdt_ref, p_ref, dt_ref, h_ref):
    j = pl.program_id(1)

    @pl.when(j == 0)
    def _():
        h = _rms(x_ref[...], gain_ref[...]).astype(BF16)
        h_ref[...] = h
        dt = _dot_nt(h, wdt_ref[...].astype(BF16))
        lane = lax.broadcasted_iota(jnp.int32, dt.shape, 1)
        dt_ref[...] = jnp.where(lane < SSM_N_HEADS, dt, 0.0)

    p_ref[...] = _dot_nt(h_ref[...], w_ref[...]).astype(BF16)


def _inproj(x, gain, wpt, wit, tm, tn):
    m, d = x.shape
    return pl.pallas_call(
        _inproj_kernel,
        grid=(m // tm, P_COLS // tn),
        in_specs=[
            pl.BlockSpec((tm, d), lambda i, j: (i, 0)),
            pl.BlockSpec((1, d), lambda i, j: (0, 0)),
            pl.BlockSpec((tn, d), lambda i, j: (j, 0)),
            pl.BlockSpec((LANES, d), lambda i, j: (W_DT // LANES, 0)),
        ],
        out_specs=[
            pl.BlockSpec((tm, tn), lambda i, j: (i, j)),
            pl.BlockSpec((tm, LANES), lambda i, j: (i, 0)),
        ],
        out_shape=[
            jax.ShapeDtypeStruct((m, P_COLS), BF16),
            jax.ShapeDtypeStruct((m, LANES), F32),
        ],
        scratch_shapes=[pltpu.VMEM((tm, d), BF16)],
        compiler_params=_params(("parallel", "arbitrary")),
        name="in_proj",
    )(x, gain, wpt, wit)


SSD_GROUPS_PER_STEP = 8


def _ssd_kernel(z_ref, x_ref, b_ref, c_ref, dt_ref,
                cwx_ref, cwb_ref, cwc_ref, cbx_ref, cbb_ref, cbc_ref,
                dtb_ref, alog_ref, dsk_ref, nrm_ref, e_ref, selt_ref,
                o_ref, ext_ref, state_ref):
    GW = SSM_GROUP_WIDTH
    N = SSM_D_STATE
    for gi in range(SSD_GROUPS_PER_STEP):
        def part(ref, width, gi=gi):
            return ref.at[:, gi * width:(gi + 1) * width]
        _ssd_group(part(z_ref, GW), part(x_ref, GW), part(b_ref, N), part(c_ref, N), dt_ref,
                   part(cwx_ref, GW), part(cwb_ref, N), part(cwc_ref, N),
                   part(cbx_ref, GW), part(cbb_ref, N), part(cbc_ref, N),
                   dtb_ref, alog_ref, part(dsk_ref, GW), part(nrm_ref, GW), e_ref.at[gi], selt_ref.at[gi],
                   part(o_ref, GW), ext_ref.at[gi], state_ref.at[gi])


def _ssd_group(z_ref, x_ref, b_ref, c_ref, dt_ref,
               cwx_ref, cwb_ref, cwc_ref, cbx_ref, cbb_ref, cbc_ref,
               dtb_ref, alog_ref, dsk_ref, nrm_ref, e_ref, selt_ref,
               o_ref, ext_ref, state_ref):
    c = pl.program_id(2)
    L = SSM_CHUNK
    GW = SSM_GROUP_WIDTH
    N = SSM_D_STATE
    XBC = GW + 2 * N

    @pl.when(c == 0)
    def _():
        ext_ref[0:CONV_HALO, :] = jnp.zeros((CONV_HALO, XBC), F32)
        state_ref[...] = jnp.zeros_like(state_ref)

    ext_ref[CONV_HALO:CONV_HALO + L, 0:GW] = x_ref[...].astype(F32)
    ext_ref[CONV_HALO:CONV_HALO + L, GW:GW + N] = b_ref[...].astype(F32)
    ext_ref[CONV_HALO:CONV_HALO + L, GW + N:XBC] = c_ref[...].astype(F32)

    cw = jnp.concatenate([cwx_ref[...], cwb_ref[...], cwc_ref[...]], axis=1)
    cbias = jnp.concatenate([cbx_ref[...], cbb_ref[...], cbc_ref[...]], axis=1)
    ext = ext_ref[...]
    acc = cw[0:1, :] * ext
    for k in range(1, SSM_CONV):
        acc = pltpu.roll(acc, 1, axis=0) + cw[k:k + 1, :] * ext
    ext_ref[0:CONV_HALO, :] = ext_ref[L:L + CONV_HALO, :]
    xbc = _silu(acc[CONV_HALO:, :] + cbias)
    xs = xbc[:, 0:GW]
    bm = xbc[:, GW:GW + N]
    cm = xbc[:, GW + N:XBC]

    dtr = dt_ref[...] + dtb_ref[...]
    dt = jnp.maximum(dtr, 0.0) + jnp.log1p(jnp.exp(-jnp.abs(dtr)))
    e01 = e_ref[...]
    dt_x = _sel_right(dt, e01)
    adt_c = dt * (-LOG2E * jnp.exp(alog_ref[...]))
    row = lax.broadcasted_iota(jnp.int32, (L, L), 0)
    col = lax.broadcasted_iota(jnp.int32, (L, L), 1)
    tril = row >= col
    tril01 = tril.astype(BF16)
    acum_c = _sel_left(tril01, adt_c)
    selt = selt_ref[...]
    hi, mid, lo = _split3(acum_c)
    acum_r = (_dot_nt(selt, lo) + _dot_nt(selt, mid)) + _dot_nt(selt, hi)
    acum_x = (_dot(lo, e01) + _dot(mid, e01)) + _dot(hi, e01)

    bt = bm.T.astype(BF16)
    cmb = cm.astype(BF16)
    H = L // 2
    cb_top = _dot(cmb[0:H, :], bt[:, 0:H])
    cb_bot = _dot(cmb[H:L, :], bt)
    xdt = xs * dt_x
    xdt_b = xdt.astype(BF16)

    tri = tril[0:H, 0:H]
    lane = lax.broadcasted_iota(jnp.int32, (L, LANES), 1)
    low_half = lane < SSM_HEAD_DIM
    pairs = []
    for pair in range(SSM_HEADS_PER_GROUP // 2):
        xp = xdt_b[:, pair * LANES:(pair + 1) * LANES]
        y_top = None
        y_bot = None
        for half in range(2):
            h = 2 * pair + half
            a_col = acum_x[:, h * SSM_HEAD_DIM:h * SSM_HEAD_DIM + 1]
            a_row = acum_r[h:h + 1, :]
            w00 = cb_top * jnp.exp2(jnp.where(tri, a_col[0:H] - a_row[:, 0:H], NEG))
            w10 = cb_bot[:, 0:H] * jnp.exp2(a_col[H:L] - a_row[:, 0:H])
            w11 = cb_bot[:, H:L] * jnp.exp2(jnp.where(tri, a_col[H:L] - a_row[:, H:L], NEG))
            keep = low_half if half == 0 else jnp.logical_not(low_half)
            xh = jnp.where(keep, xp, jnp.zeros_like(xp))
            yt = _dot(w00.astype(BF16), xh[0:H, :])
            yb = _dot(jnp.concatenate([w10, w11], axis=1).astype(BF16), xh)
            y_top = yt if y_top is None else y_top + yt
            y_bot = yb if y_bot is None else y_bot + yb
        pairs.append(jnp.concatenate([y_top, y_bot], axis=0))
    y_diag = jnp.concatenate(pairs, axis=1)

    state = state_ref[...]
    y_off = jnp.exp2(acum_x) * _dot(cmb, state.astype(BF16))
    y = y_diag + y_off + dsk_ref[...] * xs

    last = acum_x[L - 1:L, :]
    xdec = (xdt * jnp.exp2(last - acum_x)).astype(BF16)
    state_ref[...] = state * jnp.exp2(last) + _dot(bt, xdec)

    yg = y * _silu(z_ref[...].astype(F32))
    o_ref[...] = _rms(yg, nrm_ref[...]).astype(BF16)


def _ssd(p, dt_raw, conv_w, conv_b, dtb_row, alog_row, dskip_x, nrm_row, e01, selt, bsz, t_len):
    L = SSM_CHUNK
    nc = t_len // L
    G = SSD_GROUPS_PER_STEP
    GW = G * SSM_GROUP_WIDTH
    N = G * SSM_D_STATE
    rowblk = lambda b, g, c: b * nc + c
    return pl.pallas_call(
        _ssd_kernel,
        grid=(bsz, SSM_N_GROUPS // G, nc),
        in_specs=[
            pl.BlockSpec((L, GW), lambda b, g, c: (rowblk(b, g, c), PA_Z // GW + g)),
            pl.BlockSpec((L, GW), lambda b, g, c: (rowblk(b, g, c), PA_X // GW + g)),
            pl.BlockSpec((L, N), lambda b, g, c: (rowblk(b, g, c), PA_B // N + g)),
            pl.BlockSpec((L, N), lambda b, g, c: (rowblk(b, g, c), PA_C // N + g)),
            pl.BlockSpec((L, LANES), lambda b, g, c: (rowblk(b, g, c), 0)),
            pl.BlockSpec((SSM_CONV, GW), lambda b, g, c: (0, g)),
            pl.BlockSpec((SSM_CONV, N), lambda b, g, c: (0, SSM_D_INNER // N + g)),
            pl.BlockSpec((SSM_CONV, N), lambda b, g, c: (0, (SSM_D_INNER + SSM_GN) // N + g)),
            pl.BlockSpec((1, GW), lambda b, g, c: (0, g)),
            pl.BlockSpec((1, N), lambda b, g, c: (0, SSM_D_INNER // N + g)),
            pl.BlockSpec((1, N), lambda b, g, c: (0, (SSM_D_INNER + SSM_GN) // N + g)),
            pl.BlockSpec((1, LANES), lambda b, g, c: (0, 0)),
            pl.BlockSpec((1, LANES), lambda b, g, c: (0, 0)),
            pl.BlockSpec((1, GW), lambda b, g, c: (0, g)),
            pl.BlockSpec((1, GW), lambda b, g, c: (0, g)),
            pl.BlockSpec((G, LANES, SSM_GROUP_WIDTH), lambda b, g, c: (g, 0, 0)),
            pl.BlockSpec((G, SSM_HEADS_PER_GROUP, LANES), lambda b, g, c: (g, 0, 0)),
        ],
        out_specs=pl.BlockSpec((L, GW), lambda b, g, c: (rowblk(b, g, c), g)),
        out_shape=jax.ShapeDtypeStruct((bsz * t_len, SSM_D_INNER), BF16),
        scratch_shapes=[
            pltpu.VMEM((G, CONV_HALO + L, SSM_GROUP_WIDTH + 2 * SSM_D_STATE), F32),
            pltpu.VMEM((G, SSM_D_STATE, SSM_GROUP_WIDTH), F32),
        ],
        compiler_params=_params(("parallel", "parallel", "arbitrary")),
        name="ssd",
    )(p, p, p, p, dt_raw, conv_w, conv_w, conv_w, conv_b, conv_b, conv_b,
      dtb_row, alog_row, dskip_x, nrm_row, e01, selt)


def _half_rms_scale(x, low):
    sq = x * x
    zero = jnp.zeros_like(sq)
    ss_lo = jnp.sum(jnp.where(low, sq, zero), axis=-1, keepdims=True)
    ss_hi = jnp.sum(jnp.where(low, zero, sq), axis=-1, keepdims=True)
    inv = 1.0 / ATTN_HEAD_DIM
    return jnp.where(low, lax.rsqrt(ss_lo * inv + EPS), lax.rsqrt(ss_hi * inv + EPS))


RIDER_ROWS = 16


def _rider_span(rows, n_steps, max_span):
    span = 1
    while (rows * span) % (n_steps * RIDER_ROWS) != 0:
        span *= 2
        assert span <= max_span, (rows, n_steps)
    return span


ATTN_BLOCKS_PER_STEP = 2


def _attn_block(has_prev, r0, kf, vf, sink_ref, slope_ref, q_ref, qn_ref, kn_ref, o_ref):
    blk = ATTN_BLOCK
    hd = ATTN_HEAD_DIM
    sj = lax.broadcasted_iota(jnp.int32, (2 * blk, blk), 0)
    qi = lax.broadcasted_iota(jnp.int32, (2 * blk, blk), 1)
    dist = qi + blk - sj
    valid = (dist >= 0) & (dist < WINDOW)
    if has_prev is not True:
        valid = valid & ((sj >= blk) | has_prev)
    ndm = jnp.where(valid, -dist.astype(F32), NEG)
    npair = ATTN_Q_PER_KV // 2
    ndm4 = jnp.concatenate([ndm] * npair, axis=1)
    low_k = lax.broadcasted_iota(jnp.int32, (2 * blk, LANES), 1) < hd
    low_q = lax.broadcasted_iota(jnp.int32, (npair * blk, LANES), 1) < hd
    qn = qn_ref[...] * (LOG2E * ATTN_HEAD_DIM ** -0.5)
    kn = kn_ref[...]
    zk = jnp.zeros((2 * blk, LANES), F32)
    for kvp in range(ATTN_N_KV // 2):
        k2 = kf[:, kvp * LANES:(kvp + 1) * LANES]
        k2 = k2 * _half_rms_scale(k2, low_k) * kn
        v2 = vf[:, kvp * LANES:(kvp + 1) * LANES]
        k2r = pltpu.roll(k2, hd, axis=1)
        v2r = pltpu.roll(v2, hd, axis=1)
        for half in range(2):
            kv = 2 * kvp + half
            k_lo_src, k_hi_src = (k2, k2r) if half == 0 else (k2r, k2)
            v_lo_src, v_hi_src = (v2, v2r) if half == 0 else (v2r, v2)
            k_lo = jnp.where(low_k, k_lo_src, zk).astype(BF16)
            k_hi = jnp.where(low_k, zk, k_hi_src).astype(BF16)
            v_lo = jnp.where(low_k, v_lo_src, zk).astype(BF16)
            v_hi = jnp.where(low_k, zk, v_hi_src).astype(BF16)
            q4 = jnp.concatenate(
                [q_ref[r0:r0 + blk, (kv * npair + j) * LANES:(kv * npair + j + 1) * LANES] for j in range(npair)],
                axis=0).astype(F32)
            qb = (q4 * _half_rms_scale(q4, low_q) * qn).astype(BF16)
            o4 = None
            for e in range(2):
                u = 2 * kv + e
                st = _dot_nt(k_lo if e == 0 else k_hi, qb) + slope_ref[u:u + 1, :] * ndm4
                sink = sink_ref[u:u + 1, :] * LOG2E
                m = jnp.maximum(jnp.max(st, axis=0, keepdims=True), sink)
                p = jnp.exp2(st - m)
                denom = jnp.sum(p, axis=0, keepdims=True) + jnp.exp2(sink - m)
                pn = (p * (1.0 / denom)).astype(BF16)
                oe = lax.dot_general(pn, v_lo if e == 0 else v_hi, (((0,), (0,)), ((), ())),
                                     preferred_element_type=F32)
                o4 = oe if o4 is None else o4 + oe
            for j in range(npair):
                hp = kv * npair + j
                o_ref[r0:r0 + blk, hp * LANES:(hp + 1) * LANES] = o4[j * blk:(j + 1) * blk, :].astype(BF16)


def _attn_kernel(sink_ref, slope_ref, q_ref, kc_ref, kp_ref, vc_ref, vp_ref, qn_ref, kn_ref, *refs, rider_spans):
    nr = len(rider_spans)
    r_refs, o_ref, q_refs = refs[:nr], refs[nr], refs[nr + 1:]
    n = pl.program_id(1)
    blk = ATTN_BLOCK
    k_all = jnp.concatenate([kp_ref[...], kc_ref[...]], axis=0).astype(F32)
    v_all = jnp.concatenate([vp_ref[...], vc_ref[...]], axis=0).astype(F32)
    for sub in range(ATTN_BLOCKS_PER_STEP):
        has_prev = (n > 0) if sub == 0 else True
        _attn_block(has_prev, sub * blk, k_all[sub * blk:(sub + 2) * blk, :], v_all[sub * blk:(sub + 2) * blk, :],
                    sink_ref, slope_ref, q_ref, qn_ref, kn_ref, o_ref)

    for r_ref, q_ref, span in zip(r_refs, q_refs, rider_spans):
        if span == 1:
            q_ref[...] = r_ref[...].astype(BF16)
        else:
            @pl.when(n % span == 0)
            def _(r_ref=r_ref, q_ref=q_ref):
                q_ref[...] = r_ref[...].astype(BF16)


def _head_table(per_head):
    npair = ATTN_Q_PER_KV // 2
    t = per_head.astype(F32).reshape(ATTN_N_KV, npair, 2).transpose(0, 2, 1)
    return jnp.repeat(t.reshape(2 * ATTN_N_KV, npair), LANES, axis=1)


def _attn(p, sinks, qn_row, kn_row, riders, bsz, t_len):
    blk = ATTN_BLOCK
    slopes = jnp.asarray(
        [2.0 ** (-8.0 * (h + 1) / ATTN_N_HEADS) for h in range(ATTN_N_HEADS)], F32)
    per = ATTN_BLOCKS_PER_STEP
    nb = t_len // (per * blk)
    cur = lambda b, n: b * nb + n
    prev = lambda b, n: b * nb * per + jnp.maximum(n * per - 1, 0)
    n_steps = bsz * nb
    spans = tuple(_rider_span(r.shape[0], n_steps, nb) for r in riders)
    rider_specs = [
        pl.BlockSpec((r.shape[0] * span // n_steps, r.shape[1]), lambda b, n, span=span: (cur(b, n) // span, 0))
        for r, span in zip(riders, spans)]
    outs = pl.pallas_call(
        functools.partial(_attn_kernel, rider_spans=spans),
        grid=(bsz, nb),
        in_specs=[
            pl.BlockSpec((2 * ATTN_N_KV, 4 * LANES), lambda b, n: (0, 0)),
            pl.BlockSpec((2 * ATTN_N_KV, 4 * LANES), lambda b, n: (0, 0)),
            pl.BlockSpec((per * blk, ATTN_D), lambda b, n: (cur(b, n), PB_Q // ATTN_D)),
            pl.BlockSpec((per * blk, ATTN_KV_D), lambda b, n: (cur(b, n), PB_K // ATTN_KV_D)),
            pl.BlockSpec((blk, ATTN_KV_D), lambda b, n: (prev(b, n), PB_K // ATTN_KV_D)),
            pl.BlockSpec((per * blk, ATTN_KV_D), lambda b, n: (cur(b, n), PB_V // ATTN_KV_D)),
            pl.BlockSpec((blk, ATTN_KV_D), lambda b, n: (prev(b, n), PB_V // ATTN_KV_D)),
            pl.BlockSpec((1, LANES), lambda b, n: (0, 0)),
            pl.BlockSpec((1, LANES), lambda b, n: (0, 0)),
        ] + rider_specs,
        out_specs=[pl.BlockSpec((per * blk, ATTN_D), lambda b, n: (cur(b, n), 0))] + rider_specs,
        out_shape=[jax.ShapeDtypeStruct((bsz * t_len, ATTN_D), BF16)]
        + [jax.ShapeDtypeStruct(r.shape, BF16) for r in riders],
        compiler_params=_params(("parallel", "arbitrary")),
        name="swa",
    )(_head_table(sinks), _head_table(slopes * LOG2E), p, p, p, p, p, qn_row, kn_row, *riders)
    return outs[0], outs[1:]


def _merge_kernel(ys_ref, ya_ref, gs_ref, ga_ref, wos_ref, woa_ref, o_ref):
    ms = _dot(ys_ref[...], wos_ref[...])
    ma = _dot(ya_ref[...], woa_ref[...])
    mg = (jax.nn.sigmoid(gs_ref[...].astype(F32)) * ms
          + jax.nn.sigmoid(ga_ref[...].astype(F32)) * ma)
    o_ref[...] = mg.astype(BF16)


def _merge(ys, ya, p, wos, woa, tm, tn):
    m = ys.shape[0]
    d = wos.shape[1]
    return pl.pallas_call(
        _merge_kernel,
        grid=(m // tm, d // tn),
        in_specs=[
            pl.BlockSpec((tm, SSM_D_INNER), lambda i, j: (i, 0)),
            pl.BlockSpec((tm, ATTN_D), lambda i, j: (i, 0)),
            pl.BlockSpec((tm, tn), lambda i, j: (i, PB_GS // tn + j)),
            pl.BlockSpec((tm, tn), lambda i, j: (i, PB_GA // tn + j)),
            pl.BlockSpec((SSM_D_INNER, tn), lambda i, j: (0, j)),
            pl.BlockSpec((ATTN_D, tn), lambda i, j: (0, j)),
        ],
        out_specs=pl.BlockSpec((tm, tn), lambda i, j: (i, j)),
        out_shape=jax.ShapeDtypeStruct((m, d), BF16),
        compiler_params=_params(("parallel", "arbitrary")),
        name="merge",
    )(ys, ya, p, p, wos, woa)


def _outproj_kernel(x_ref, m_ref, w_ref, o_ref, wb_ref):
    @pl.when(pl.program_id(0) == 0)
    def _():
        wb_ref[...] = w_ref[...].astype(BF16)

    o_ref[...] = x_ref[...] + _dot(m_ref[...], wb_ref[...])


def _outproj(x, mg, wout, tm):
    m, d = x.shape
    return pl.pallas_call(
        _outproj_kernel,
        grid=(m // tm,),
        in_specs=[
            pl.BlockSpec((tm, d), lambda i: (i, 0)),
            pl.BlockSpec((tm, d), lambda i: (i, 0)),
            pl.BlockSpec((d, d), lambda i: (0, 0), pipeline_mode=pl.Buffered(1)),
        ],
        out_specs=pl.BlockSpec((tm, d), lambda i: (i, 0)),
        out_shape=jax.ShapeDtypeStruct((m, d), F32),
        scratch_shapes=[pltpu.VMEM((d, d), BF16)],
        compiler_params=_params(("arbitrary",)),
        name="out_proj",
    )(x, mg, wout)


def _selection_constants():
    e = np.zeros((SSM_N_GROUPS, LANES, SSM_GROUP_WIDTH), np.float32)
    st = np.zeros((SSM_N_GROUPS, SSM_HEADS_PER_GROUP, LANES), np.float32)
    for g in range(SSM_N_GROUPS):
        for h in range(SSM_HEADS_PER_GROUP):
            e[g, g * SSM_HEADS_PER_GROUP + h, h * SSM_HEAD_DIM:(h + 1) * SSM_HEAD_DIM] = 1.0
            st[g, h, g * SSM_HEADS_PER_GROUP + h] = 1.0
    return jnp.asarray(e, BF16), jnp.asarray(st, BF16)


def _pad_lanes(v):
    return jnp.pad(v.astype(F32), (0, LANES - v.shape[0])).reshape(1, LANES)


def kernel(x, ffn1_norm, ffn1_w_gate, ffn1_w_up, ffn1_w_down, mix_norm, w_in, conv_w, conv_b, dt_bias, a_log, d_skip, ssm_norm, q_norm, k_norm, sinks, w_o_ssm, w_o_attn, w_out, ffn2_norm, ffn2_w_gate, ffn2_w_up, ffn2_w_down):
    bsz, t_len, d = x.shape
    m = bsz * t_len
    depth = ffn1_norm.shape[0]
    e01, selt = _selection_constants()
    xf = x.reshape(m, d)
    for l in range(depth):
        wit = jnp.swapaxes(w_in[l], 0, 1)

        xf, wpt = _ffn(xf, ffn1_norm[l].reshape(1, d), ffn1_w_gate[l].astype(BF16),
                       ffn1_w_up[l].astype(BF16), ffn1_w_down[l].astype(BF16), tm=TM, tf=FFN_TF, wit=wit)

        p, dt_raw = _inproj(xf, mix_norm[l].reshape(1, d), wpt, wit, tm=TM, tn=INPROJ_TN)

        y_ssm = _ssd(
            p, dt_raw, conv_w[l], conv_b[l].reshape(1, -1),
            _pad_lanes(dt_bias[l]), _pad_lanes(a_log[l]),
            jnp.repeat(d_skip[l].astype(F32), SSM_HEAD_DIM).reshape(1, SSM_D_INNER),
            ssm_norm[l].reshape(1, SSM_D_INNER), e01, selt, bsz, t_len)

        y_attn, (wos, woa, wg2, wu2, wd2) = _attn(
            p, sinks[l].astype(F32),
            jnp.tile(q_norm[l].astype(F32), LANES // ATTN_HEAD_DIM).reshape(1, LANES),
            jnp.tile(k_norm[l].astype(F32), LANES // ATTN_HEAD_DIM).reshape(1, LANES),
            (w_o_ssm[l], w_o_attn[l], ffn2_w_gate[l], ffn2_w_up[l], ffn2_w_down[l]), bsz, t_len)

        mg = _merge(y_ssm, y_attn, p, wos, woa, tm=TM, tn=MERGE_TN)
        xf = _outproj(xf, mg, w_out[l], tm=OUTPROJ_TM)

        xf = _ffn(xf, ffn2_norm[l].reshape(1, d), wg2, wu2, wd2, tm=TM, tf=FFN_TF)
    return xf.reshape(bsz, t_len, d)
```
